```python
import math
import jax, jax.numpy as jnp
from jax import lax
import numpy as np

D_MODEL = 1024
BATCH = 1
SEQ = 16384
DEPTH = 4
DEC_BATCH = 8
DEC_SEQ = 32
PAST_LEN = 4096

CHUNK = 64
Q_BLOCK = 128
H_A = 8
NOPE_DIM = 64
ROPE_DIM = 32
V_DIM_A = 64
Q_LORA = 256
KV_LORA = 256
ROPE_THETA = 10000.0
H_B = 8
DH_B = 64
N_EXPERTS = 32
TOP_K = 4
D_FF = 1024
SWIGLU_LIMIT = 7.0
SWIGLU_ALPHA = 1.702
EXPERT_BLOCK = 128
LN_EPS = 1e-5
RMS_EPS = 1e-6
ALPHA = (2 * DEPTH) ** 0.25
BETA = (8 * DEPTH) ** -0.25
IN_SPLITS = (Q_LORA, KV_LORA, ROPE_DIM, 2 * H_B * DH_B, 2 * H_B * DH_B, 2 * H_B * DH_B, D_MODEL, D_MODEL)
IN_WIDTH = sum(IN_SPLITS)

kernel_name = 'mla_diffattn_gated_moe_streaming_step'


def _layer_norm(x, g, b):
    xf = x.astype(jnp.float32)
    mu = xf.mean(-1, keepdims=True)
    var = jnp.square(xf - mu).mean(-1, keepdims=True)
    return ((xf - mu) * lax.rsqrt(var + LN_EPS) * g.astype(jnp.float32) + b.astype(jnp.float32)).astype(x.dtype)


def _rms_norm(x, g):
    xf = x.astype(jnp.float32)
    ms = jnp.square(xf).mean(-1, keepdims=True)
    return (xf * lax.rsqrt(ms + RMS_EPS) * g.astype(jnp.float32)).astype(x.dtype)


def _rope(x, pos):
    half = ROPE_DIM // 2
    inv = ROPE_THETA ** (-jnp.arange(half, dtype=jnp.float32) / half)
    ang = pos.astype(jnp.float32)[:, None] * inv[None, :]
    cos, sin = jnp.cos(ang), jnp.sin(ang)
    if x.ndim == 4:
        cos, sin = cos[:, None, :], sin[:, None, :]
    xf = x.astype(jnp.float32)
    x1, x2 = xf[..., :half], xf[..., half:]
    return jnp.concatenate([x1 * cos - x2 * sin, x2 * cos + x1 * sin], axis=-1).astype(x.dtype)


def _chunk_mask(q_pos, k_pos):
    return (k_pos[None, :] // CHUNK) <= (q_pos[:, None] // CHUNK)


def _alibi_slopes():
    return 2.0 ** (-8.0 * (jnp.arange(H_B, dtype=jnp.float32) + 1.0) / H_B)


def _split_in(z):
    out, start = [], 0
    for w in IN_SPLITS:
        out.append(z[..., start:start + w])
        start += w
    return out


def _sweep_query_blocks(fn, qs, q_pos):
    S = q_pos.shape[0]
    if S <= Q_BLOCK or S % Q_BLOCK != 0:
        return fn(qs, q_pos)
    nb = S // Q_BLOCK
    qb = jax.tree_util.tree_map(
        lambda t: jnp.moveaxis(t.reshape(t.shape[0], nb, Q_BLOCK, *t.shape[2:]), 1, 0), qs)
    out = lax.map(lambda a: fn(a[0], a[1]), (qb, q_pos.reshape(nb, Q_BLOCK)))
    out = jnp.moveaxis(out, 0, 1)
    return out.reshape(out.shape[0], S, *out.shape[3:])


def _mla_attend(q, k, v, q_pos, k_pos):
    s = jnp.einsum('bqhd,bkhd->bhqk', q, k, preferred_element_type=jnp.float32) * (NOPE_DIM + ROPE_DIM) ** -0.5
    p = jax.nn.softmax(jnp.where(_chunk_mask(q_pos, k_pos), s, -jnp.inf), axis=-1)
    return jnp.einsum('bhqk,bkhd->bqhd', p.astype(v.dtype), v)


def _diff_attend(q1, q2, k1, k2, v, lam, q_pos, k_pos):
    mask = _chunk_mask(q_pos, k_pos)
    dist = jnp.abs(q_pos[:, None] - k_pos[None, :]).astype(jnp.float32)
    bias = -_alibi_slopes()[:, None, None] * dist

    def probs(q, k):
        s = jnp.einsum('bqhd,bkhd->bhqk', q, k, preferred_element_type=jnp.float32) * DH_B ** -0.5 + bias
        return jax.nn.softmax(jnp.where(mask, s, -jnp.inf), axis=-1)

    a = probs(q1, k1) - lam * probs(q2, k2)
    return jnp.einsum('bhqk,bkhd->bqhd', a.astype(v.dtype), v)


def _token_mixer(x, past, q_pos, k_pos, lam_init, w_in, q_norm_g, kv_norm_g, w_uq, w_ukv,
                 lam_q1, lam_k1, lam_q2, lam_k2, subln_g, w_oa, w_ob, w_out):
    B, S, _ = x.shape
    cq, ckv, kr, qd, kd, vd, ga, gb = _split_in(x @ w_in)
    cq = _rms_norm(cq, q_norm_g)
    ckv = _rms_norm(ckv, kv_norm_g)
    kr = _rope(kr, q_pos)
    qa = (cq @ w_uq).reshape(B, S, H_A, NOPE_DIM + ROPE_DIM)
    qa = jnp.concatenate([qa[..., :NOPE_DIM], _rope(qa[..., NOPE_DIM:], q_pos)], axis=-1)
    qd = qd.reshape(B, S, 2 * H_B, DH_B)
    kd = kd.reshape(B, S, 2 * H_B, DH_B)
    vd = vd.reshape(B, S, H_B, 2 * DH_B)
    if past is None:
        lat_all, kr_all, kd_all, vd_all = ckv, kr, kd, vd
    else:
        lat_all = jnp.concatenate([past[0], ckv], axis=1)
        kr_all = jnp.concatenate([past[1], kr], axis=1)
        kd_all = jnp.concatenate([past[2], kd], axis=1)
        vd_all = jnp.concatenate([past[3], vd], axis=1)
    Sk = lat_all.shape[1]
    kv = (lat_all @ w_ukv).reshape(B, Sk, H_A, NOPE_DIM + V_DIM_A)
    ka = jnp.concatenate(
        [kv[..., :NOPE_DIM], jnp.broadcast_to(kr_all[:, :, None, :], (B, Sk, H_A, ROPE_DIM))], axis=-1)
    va = kv[..., NOPE_DIM:]
    oa = _sweep_query_blocks(lambda qs, qp: _mla_attend(qs[0], ka, va, qp, k_pos), (qa,), q_pos)
    lam = (jnp.exp(jnp.sum(lam_q1.astype(jnp.float32) * lam_k1.astype(jnp.float32)))
           - jnp.exp(jnp.sum(lam_q2.astype(jnp.float32) * lam_k2.astype(jnp.float32))) + lam_init)
    k1, k2 = kd_all[:, :, 0::2], kd_all[:, :, 1::2]
    ob = _sweep_query_blocks(
        lambda qs, qp: _diff_attend(qs[0], qs[1], k1, k2, vd_all, lam, qp, k_pos),
        (qd[:, :, 0::2], qd[:, :, 1::2]), q_pos)
    ob = _rms_norm(ob, subln_g) * (1.0 - lam_init)
    merged = (jax.nn.sigmoid(ga) * (oa.reshape(B, S, H_A * V_DIM_A) @ w_oa)
              + jax.nn.sigmoid(gb) * (ob.reshape(B, S, 2 * H_B * DH_B) @ w_ob))
    return merged @ w_out, (ckv, kr, kd, vd)


def _moe(x, w_router, b_router, w_gu, b_gu, w_dn, b_dn):
    B, S, D = x.shape
    T = B * S
    xt = x.reshape(T, D)
    logits = jnp.dot(xt, w_router, preferred_element_type=jnp.float32) + b_router.astype(jnp.float32)
    top_val, top_idx = lax.top_k(logits, TOP_K)
    gates = jax.nn.softmax(top_val, axis=-1)
    n_slots = T * TOP_K
    flat_e = top_idx.reshape(-1).astype(jnp.int32)
    order = jnp.argsort(flat_e)
    sorted_e = flat_e[order]
    counts = jnp.bincount(flat_e, length=N_EXPERTS).astype(jnp.int32)
    padded = (counts + EXPERT_BLOCK - 1) // EXPERT_BLOCK * EXPERT_BLOCK
    pad_end = jnp.cumsum(padded)
    pad_start = pad_end - padded
    grp_start = jnp.cumsum(counts) - counts
    dest = pad_start[sorted_e] + jnp.arange(n_slots, dtype=jnp.int32) - grp_start[sorted_e]
    n_blocks = -(-n_slots // EXPERT_BLOCK) + N_EXPERTS
    n_rows = n_blocks * EXPERT_BLOCK
    row_tok = jnp.full((n_rows,), T, jnp.int32).at[dest].set((order // TOP_K).astype(jnp.int32))
    row_gate = jnp.zeros((n_rows,), jnp.float32).at[dest].set(gates.reshape(-1)[order])
    block_e = jnp.minimum(
        jnp.searchsorted(pad_end, jnp.arange(n_blocks, dtype=jnp.int32) * EXPERT_BLOCK, side='right'),
        N_EXPERTS - 1)
    xr = jnp.concatenate([xt, jnp.zeros((1, D), xt.dtype)], axis=0)[row_tok]
    xr = xr.reshape(n_blocks, EXPERT_BLOCK, D)

    def expert_rows(a):
        xb, e = a
        h = xb @ w_gu[e] + b_gu[e]
        g = jnp.minimum(h[:, :D_FF], SWIGLU_LIMIT)
        u = jnp.clip(h[:, D_FF:], -SWIGLU_LIMIT, SWIGLU_LIMIT)
        act = (u + 1.0) * (g * jax.nn.sigmoid(SWIGLU_ALPHA * g))
        return act @ w_dn[e] + b_dn[e]

    yr = lax.map(expert_rows, (xr, block_e)).reshape(n_rows, D)
    out = jnp.zeros((T + 1, D), jnp.float32).at[row_tok].add(yr.astype(jnp.float32) * row_gate[:, None])
    return out[:T].astype(x.dtype).reshape(B, S, D)


def setup_inputs(seed: int = 0) -> dict:
    key = jax.random.key(seed)
    ks = jax.random.split(key, 32)
    f32 = jnp.float32
    L = DEPTH

    def nrm(k, shape, scale):
        return jax.random.normal(k, shape, f32) * scale

    def gain(k, shape):
        return 1.0 + 0.05 * jax.random.normal(k, shape, f32)

    return {
        'x_prompt': nrm(ks[0], (BATCH, SEQ, D_MODEL), 1.0),
        'x_sample': nrm(ks[1], (DEC_BATCH, DEC_SEQ, D_MODEL), 1.0),
        'cache_mla_latent': nrm(ks[2], (L, DEC_BATCH, PAST_LEN, KV_LORA), 1.0),
        'cache_mla_krope': nrm(ks[3], (L, DEC_BATCH, PAST_LEN, ROPE_DIM), 1.0),
        'cache_diff_k': nrm(ks[4], (L, DEC_BATCH, PAST_LEN, 2 * H_B, DH_B), 1.0),
        'cache_diff_v': nrm(ks[5], (L, DEC_BATCH, PAST_LEN, H_B, 2 * DH_B), 1.0),
        'ln_in_g': gain(ks[6], (D_MODEL,)),
        'ln_in_b': nrm(ks[7], (D_MODEL,), 0.02),
        'w_in': nrm(ks[8], (L, D_MODEL, IN_WIDTH), D_MODEL ** -0.5),
        'q_norm_g': gain(ks[9], (L, Q_LORA)),
        'kv_norm_g': gain(ks[10], (L, KV_LORA)),
        'w_uq': nrm(ks[11], (L, Q_LORA, H_A * (NOPE_DIM + ROPE_DIM)), Q_LORA ** -0.5),
        'w_ukv': nrm(ks[12], (L, KV_LORA, H_A * (NOPE_DIM + V_DIM_A)), KV_LORA ** -0.5),
        'lam_q1': nrm(ks[13], (L, DH_B), 0.1),
        'lam_k1': nrm(ks[14], (L, DH_B), 0.1),
        'lam_q2': nrm(ks[15], (L, DH_B), 0.1),
        'lam_k2': nrm(ks[16], (L, DH_B), 0.1),
        'subln_g': gain(ks[17], (L, 2 * DH_B)),
        'w_oa': nrm(ks[18], (L, H_A * V_DIM_A, D_MODEL), (H_A * V_DIM_A) ** -0.5),
        'w_ob': nrm(ks[19], (L, 2 * H_B * DH_B, D_MODEL), (2 * H_B * DH_B) ** -0.5),
        'w_out': nrm(ks[20], (L, D_MODEL, D_MODEL), BETA * D_MODEL ** -0.5),
        'ln1_g': gain(ks[21], (L, D_MODEL)),
        'ln1_b': nrm(ks[22], (L, D_MODEL), 0.02),
        'w_router': nrm(ks[23], (L, D_MODEL, N_EXPERTS), D_MODEL ** -0.5),
        'b_router': nrm(ks[24], (L, N_EXPERTS), 0.01),
        'w_gu': nrm(ks[25], (L, N_EXPERTS, D_MODEL, 2 * D_FF), D_MODEL ** -0.5),
        'b_gu': nrm(ks[26], (L, N_EXPERTS, 2 * D_FF), 0.01),
        'w_dn': nrm(ks[27], (L, N_EXPERTS, D_FF, D_MODEL), BETA * D_FF ** -0.5),
        'b_dn': nrm(ks[28], (L, N_EXPERTS, D_MODEL), 0.01),
        'ln2_g': gain(ks[29], (L, D_MODEL)),
        'ln2_b': nrm(ks[30], (L, D_MODEL), 0.02),
    }


def reference(x_prompt, x_sample, cache_mla_latent, cache_mla_krope, cache_diff_k, cache_diff_v,
              ln_in_g, ln_in_b, w_in, q_norm_g, kv_norm_g, w_uq, w_ukv, lam_q1, lam_k1, lam_q2, lam_k2,
              subln_g, w_oa, w_ob, w_out, ln1_g, ln1_b, w_router, b_router, w_gu, b_gu, w_dn, b_dn,
              ln2_g, ln2_b):
    s_prompt = x_prompt.shape[1]
    s_sample = x_sample.shape[1]
    past_len = cache_mla_latent.shape[2]
    pos_p = jnp.arange(s_prompt, dtype=jnp.int32)
    pos_sq = past_len + jnp.arange(s_sample, dtype=jnp.int32)
    pos_sk = jnp.arange(past_len + s_sample, dtype=jnp.int32)

    xp = _layer_norm(x_prompt, ln_in_g, ln_in_b)
    xs = _layer_norm(x_sample, ln_in_g, ln_in_b)
    new_p = ([], [], [], [])
    new_s = ([], [], [], [])
    for l in range(DEPTH):
        lam_init = 0.8 - 0.6 * math.exp(-0.3 * l)
        mix_w = (w_in[l], q_norm_g[l], kv_norm_g[l], w_uq[l], w_ukv[l], lam_q1[l], lam_k1[l],
                 lam_q2[l], lam_k2[l], subln_g[l], w_oa[l], w_ob[l], w_out[l])
        moe_w = (w_router[l], b_router[l], w_gu[l], b_gu[l], w_dn[l], b_dn[l])
        past = (cache_mla_latent[l], cache_mla_krope[l], cache_diff_k[l], cache_diff_v[l])

        yp, st_p = _token_mixer(xp, None, pos_p, pos_p, lam_init, *mix_w)
        ys, st_s = _token_mixer(xs, past, pos_sq, pos_sk, lam_init, *mix_w)
        for i in range(4):
            new_p[i].append(st_p[i])
            new_s[i].append(st_s[i])
        xp = _layer_norm(ALPHA * xp + yp, ln1_g[l], ln1_b[l])
        xs = _layer_norm(ALPHA * xs + ys, ln1_g[l], ln1_b[l])
        xp = _layer_norm(ALPHA * xp + _moe(xp, *moe_w), ln2_g[l], ln2_b[l])
        xs = _layer_norm(ALPHA * xs + _moe(xs, *moe_w), ln2_g[l], ln2_b[l])

    return (xp, xs,
            jnp.stack(new_p[0]), jnp.stack(new_p[1]), jnp.stack(new_p[2]), jnp.stack(new_p[3]),
            jnp.stack(new_s[0]), jnp.stack(new_s[1]), jnp.stack(new_s[2]), jnp.stack(new_s[3]))
```

```python
import functools
import math

import jax
import jax.numpy as jnp
from jax import lax
from jax.experimental import pallas as pl
from jax.experimental.pallas import tpu as pltpu

F32 = jnp.float32
BF16 = jnp.bfloat16
I32 = jnp.int32

D_MODEL = 1024
DEPTH = 4
CHUNK = 64
CHUNK_SHIFT = 6
H_A = 8
NOPE_DIM = 64
ROPE_DIM = 32
V_DIM_A = 64
Q_LORA = 256
KV_LORA = 256
ROPE_THETA = 10000.0
H_B = 8
DH_B = 64
N_EXPERTS = 32
TOP_K = 4
D_FF = 1024
SWIGLU_LIMIT = 7.0
SWIGLU_ALPHA = 1.702
LN_EPS = 1e-5
RMS_EPS = 1e-6
ALPHA = (2 * DEPTH) ** 0.25

LANE = 128
HEAD_COLS = H_A * LANE
MLA_SCALE = (NOPE_DIM + ROPE_DIM) ** -0.5
DIFF_SCALE = DH_B ** -0.5
NEG = -1e30
VMEM_LIMIT = 48 * 1024 * 1024

C_LAT = 0
C_KR = 512
C_KRROT = 640
C_QD = 768
C_KD = C_QD + 1024
C_VD = C_KD + 1024
C_GA = C_VD + 1024
C_GB = C_GA + 1024
C_END = C_GB + 1024


def _pick_tile(n, pref):
    t = min(n, pref)
    while n % t:
        t -= 8
    assert t > 0 and (t % 8 == 0 or t == n), (n, pref)
    return t


def _params(sem):
    return pltpu.CompilerParams(dimension_semantics=sem, vmem_limit_bytes=VMEM_LIMIT)


def _layer_norm(x, g, b):
    mu = jnp.mean(x, axis=-1, keepdims=True)
    xc = x - mu
    var = jnp.mean(xc * xc, axis=-1, keepdims=True)
    return xc * lax.rsqrt(var + LN_EPS) * g + b


def _rms_norm(x, g, eps):
    ms = jnp.mean(x * x, axis=-1, keepdims=True)
    return x * lax.rsqrt(ms + eps) * g


def _dot(a, b):
    return jnp.dot(a, b, preferred_element_type=F32)


def _dot_nt(a, b):
    return lax.dot_general(a, b, (((1,), (1,)), ((), ())), preferred_element_type=F32)


def _ln_body(x_ref, g_ref, b_ref, o_ref):
    o_ref[...] = _layer_norm(x_ref[...], g_ref[...], b_ref[...])


def _ln_call(x, g, b):
    t, d = x.shape
    tm = _pick_tile(t, 512)
    row = pl.BlockSpec((tm, d), lambda i: (i, 0))
    vec = pl.BlockSpec((1, d), lambda i: (0, 0))
    return pl.pallas_call(
        _ln_body,
        out_shape=jax.ShapeDtypeStruct((t, d), F32),
        grid=(t // tm,),
        in_specs=[row, vec, vec],
        out_specs=row,
        compiler_params=_params(("parallel",)),
        name="ln_in",
    )(x, g.reshape(1, d), b.reshape(1, d))


def _inproj_body(x_ref, w_ref, wuq_ref, gq_ref, gkv_ref, cos_ref, sin_ref,
                 q_ref, lat_ref, krb_ref, qd_ref, kd32_ref, kd16_ref, vd32_ref, vd16_ref, sga_ref, sgb_ref):
    xb = x_ref[...].astype(BF16)
    cos = cos_ref[...]
    sin = sin_ref[...]

    def proj(lo, hi):
        return _dot(xb, w_ref[:, lo:hi])

    z = proj(C_LAT, C_QD)
    cq = _rms_norm(z[:, :Q_LORA], gq_ref[...], RMS_EPS)
    lat_ref[...] = _rms_norm(z[:, Q_LORA:C_KR], gkv_ref[...], RMS_EPS)
    krb_ref[...] = z[:, C_KR:C_KRROT] * cos + z[:, C_KRROT:C_QD] * sin
    qq = _dot(cq.astype(BF16), wuq_ref[...])
    for h in range(H_A):
        lo = h * LANE
        qh = qq[:, lo:lo + LANE] * cos + qq[:, HEAD_COLS + lo:HEAD_COLS + lo + LANE] * sin
        q_ref[:, lo:lo + LANE] = (qh * MLA_SCALE).astype(BF16)
    qd_ref[...] = (proj(C_QD, C_KD) * DIFF_SCALE).astype(BF16)
    z = proj(C_KD, C_VD)
    kd32_ref[...] = z
    kd16_ref[...] = z.astype(BF16)
    z = proj(C_VD, C_GA)
    vd32_ref[...] = z
    vd16_ref[...] = z.astype(BF16)
    sga_ref[...] = jax.nn.sigmoid(proj(C_GA, C_GB)).astype(BF16)
    sgb_ref[...] = jax.nn.sigmoid(proj(C_GB, C_END)).astype(BF16)


def _inproj_call(x, w, wuq, gq, gkv, cos_t, sin_t):
    t, d = x.shape
    tm = _pick_tile(t, 256)
    row = lambda n: pl.BlockSpec((tm, n), lambda i: (i, 0))
    const = lambda a: pl.BlockSpec(a.shape, lambda i: (0, 0), pipeline_mode=pl.Buffered(1))
    wide = lambda dt: jax.ShapeDtypeStruct((t, HEAD_COLS), dt)
    return pl.pallas_call(
        _inproj_body,
        out_shape=(wide(BF16), jax.ShapeDtypeStruct((t, KV_LORA), F32), jax.ShapeDtypeStruct((t, LANE), F32),
                   wide(BF16), wide(F32), wide(BF16), wide(F32), wide(BF16), wide(BF16), wide(BF16)),
        grid=(t // tm,),
        in_specs=[row(d), const(w), const(wuq), const(gq), const(gkv), row(LANE), row(LANE)],
        out_specs=(row(HEAD_COLS), row(KV_LORA), row(LANE), row(HEAD_COLS), row(HEAD_COLS), row(HEAD_COLS),
                   row(HEAD_COLS), row(HEAD_COLS), row(HEAD_COLS), row(HEAD_COLS)),
        compiler_params=_params(("parallel",)),
        name="inproj",
    )(x, w, wuq, gq, gkv, cos_t, sin_t)


def _kvup_body(lat_ref, krb_ref, w_ref, k_ref, v_ref):
    kv = _dot(lat_ref[...].astype(BF16), w_ref[...])
    krb = krb_ref[...]
    for h in range(H_A):
        lo = h * LANE
        k_ref[:, lo:lo + LANE] = (kv[:, lo:lo + LANE] + krb).astype(BF16)
    v_ref[...] = kv[:, HEAD_COLS:].astype(BF16)


def _kvup_call(lat, krb, w):
    t = lat.shape[0]
    tm = _pick_tile(t, 512)
    row = lambda n: pl.BlockSpec((tm, n), lambda i: (i, 0))
    return pl.pallas_call(
        _kvup_body,
        out_shape=(jax.ShapeDtypeStruct((t, HEAD_COLS), BF16), jax.ShapeDtypeStruct((t, HEAD_COLS), BF16)),
        grid=(t // tm,),
        in_specs=[row(KV_LORA), row(LANE), pl.BlockSpec(w.shape, lambda i: (0, 0))],
        out_specs=(row(HEAD_COLS), row(HEAD_COLS)),
        compiler_params=_params(("parallel",)),
        name="kvup",
    )(lat, krb, w)


def _chunk_visible(q0, k0, shape):
    r = q0 + lax.broadcasted_iota(I32, shape, 0)
    c = k0 + lax.broadcasted_iota(I32, shape, 1)
    return (c >> CHUNK_SHIFT) <= (r >> CHUNK_SHIFT)


def _stack_diff_queries(q):
    lane = lax.broadcasted_iota(I32, q.shape, 1)
    zero = jnp.zeros_like(q)
    return jnp.concatenate([jnp.where(lane < DH_B, q, zero), jnp.where(lane >= DH_B, q, zero)], axis=0)


def _diff_lambda(lamv_ref, lam_init):
    lv = lamv_ref[...]
    a = jnp.sum(lv[0:1] * lv[1:2], axis=-1, keepdims=True)
    b = jnp.sum(lv[2:3] * lv[3:4], axis=-1, keepdims=True)
    return jnp.exp(a) - jnp.exp(b) + lam_init


def _diff_finish(o1, o2, lam, g, lam_init):
    o = o1 - lam * o2
    return _rms_norm(o, g, RMS_EPS) * (1.0 - lam_init)


def _online_update(s, v, m_sc, l_sc, acc_sc):
    m_prev = m_sc[...]
    m_new = jnp.maximum(m_prev, jnp.max(s, axis=-1, keepdims=True))
    alpha = jnp.exp(m_prev - m_new)
    p = jnp.exp(s - m_new)
    l_sc[...] = alpha * l_sc[...] + jnp.sum(p, axis=-1, keepdims=True)
    acc_sc[...] = alpha * acc_sc[...] + _dot(p.astype(BF16), v)
    m_sc[...] = m_new


def _init_state(m_sc, l_sc, acc_sc):
    m_sc[...] = jnp.full(m_sc.shape, NEG, F32)
    l_sc[...] = jnp.zeros(l_sc.shape, F32)
    acc_sc[...] = jnp.zeros(acc_sc.shape, F32)


def _sweep_key_tiles(tile, q0, tq, tk):
    n_full = q0 // tk
    lax.fori_loop(0, n_full, lambda kt, c: (tile(kt, False), c)[1], 0)
    for j in range(max(1, tq // tk)):
        tile(n_full + j, True)


def _mla_prompt_body(q_ref, k_ref, v_ref, o_ref, m_sc, l_sc, acc_sc, *, tq, tk):
    q0 = pl.program_id(1) * tq
    q = q_ref[...]
    _init_state(m_sc, l_sc, acc_sc)

    def tile(kt, masked):
        k0 = pl.multiple_of(kt * tk, tk)
        s = _dot_nt(q, k_ref[pl.ds(k0, tk), :])
        if masked:
            s = jnp.where(_chunk_visible(q0, k0, s.shape), s, NEG)
        _online_update(s, v_ref[pl.ds(k0, tk), :], m_sc, l_sc, acc_sc)

    _sweep_key_tiles(tile, q0, tq, tk)
    o_ref[...] = (acc_sc[...] / l_sc[...]).astype(o_ref.dtype)


def _diff_prompt_body(q_ref, k_ref, v_ref, slope_ref, lamv_ref, g_ref, o_ref, m_sc, l_sc, acc_sc,
                      *, tq, tk, lam_init):
    q0 = pl.program_id(1) * tq
    qs = _stack_diff_queries(q_ref[...])
    slope = slope_ref[0]
    _init_state(m_sc, l_sc, acc_sc)

    def tile(kt, masked):
        k0 = pl.multiple_of(kt * tk, tk)
        s = _dot_nt(qs, k_ref[pl.ds(k0, tk), :])
        c = (k0 - q0 + lax.broadcasted_iota(I32, (1, tk), 1)).astype(F32)
        if masked:
            a = lax.broadcasted_iota(I32, (tq, tk), 0).astype(F32)
            bias = slope[:, :1] * jnp.minimum(c, 2.0 * a - c)
            vis = _chunk_visible(q0, k0, (tq, tk))
            bias = jnp.concatenate([bias, bias], axis=0)
            vis = jnp.concatenate([vis, vis], axis=0)
            s = jnp.where(vis, s + bias, NEG)
        else:
            s = s + slope[:, :1] * c
        _online_update(s, v_ref[pl.ds(k0, tk), :], m_sc, l_sc, acc_sc)

    _sweep_key_tiles(tile, q0, tq, tk)
    o = acc_sc[...] / l_sc[...]
    lam = _diff_lambda(lamv_ref, lam_init)
    o_ref[...] = _diff_finish(o[:tq], o[tq:], lam, g_ref[...], lam_init).astype(o_ref.dtype)


def _prompt_attn_call(body, q, k, v, extra, extra_specs, *, rows_per_q, tq, tk, name):
    t = q.shape[0]
    assert t % tq == 0 and t % tk == 0 and (tq % tk == 0 or tk % tq == 0) and tq % CHUNK == 0 and tk % CHUNK == 0
    qspec = pl.BlockSpec((tq, LANE), lambda h, i: (i, h))
    kvspec = pl.BlockSpec((t, LANE), lambda h, i: (0, h))
    return pl.pallas_call(
        functools.partial(body, tq=tq, tk=tk),
        out_shape=jax.ShapeDtypeStruct((t, HEAD_COLS), BF16),
        grid=(H_A, t // tq),
        in_specs=[qspec, kvspec, kvspec] + extra_specs,
        out_specs=qspec,
        scratch_shapes=[pltpu.VMEM((rows_per_q * tq, 1), F32), pltpu.VMEM((rows_per_q * tq, 1), F32),
                        pltpu.VMEM((rows_per_q * tq, LANE), F32)],
        compiler_params=_params(("parallel", "arbitrary")),
        name=name,
    )(q, k, v, *extra)


def _two_part_softmax_pv(s_past, s_new, v_past, v_new):
    m = jnp.maximum(jnp.max(s_past, axis=-1, keepdims=True), jnp.max(s_new, axis=-1, keepdims=True))
    p_past = jnp.exp(s_past - m)
    p_new = jnp.exp(s_new - m)
    l = jnp.sum(p_past, axis=-1, keepdims=True) + jnp.sum(p_new, axis=-1, keepdims=True)
    acc = _dot(p_past.astype(BF16), v_past) + _dot(p_new.astype(BF16), v_new)
    return acc / l


def _mla_sample_body(q_ref, kp_ref, vp_ref, kn_ref, vn_ref, o_ref, *, past):
    q = q_ref[...]
    s_past = _dot_nt(q, kp_ref[...])
    s_new = _dot_nt(q, kn_ref[...])
    s_past = jnp.where(_chunk_visible(past, 0, s_past.shape), s_past, NEG)
    s_new = jnp.where(_chunk_visible(past, past, s_new.shape), s_new, NEG)
    o_ref[...] = _two_part_softmax_pv(s_past, s_new, vp_ref[...], vn_ref[...]).astype(o_ref.dtype)


def _diff_sample_body(q_ref, kp_ref, vp_ref, kn_ref, vn_ref, slope_ref, lamv_ref, g_ref, o_ref,
                      *, past, lam_init):
    sq = q_ref.shape[0]
    qs = _stack_diff_queries(q_ref[...])
    slope = slope_ref[0][:, :1]

    def scores(k, k0):
        n = k.shape[0]
        s = _dot_nt(qs, k.astype(BF16))
        qpos = past + lax.broadcasted_iota(I32, (sq, n), 0)
        kpos = k0 + lax.broadcasted_iota(I32, (sq, n), 1)
        bias = -slope * jnp.abs(qpos - kpos).astype(F32)
        vis = _chunk_visible(past, k0, (sq, n))
        bias = jnp.concatenate([bias, bias], axis=0)
        vis = jnp.concatenate([vis, vis], axis=0)
        return jnp.where(vis, s + bias, NEG)

    o = _two_part_softmax_pv(scores(kp_ref[...], 0), scores(kn_ref[...], past),
                             vp_ref[...].astype(BF16), vn_ref[...])
    lam = _diff_lambda(lamv_ref, lam_init)
    o_ref[...] = _diff_finish(o[:sq], o[sq:], lam, g_ref[...], lam_init).astype(o_ref.dtype)


def _sample_attn_call(body, q, kp, vp, kn, vn, extra, extra_specs, *, batch, past, name):
    sq = q.shape[0] // batch
    assert past % CHUNK == 0 and kp.shape[0] == batch * past
    new = pl.BlockSpec((sq, LANE), lambda b, h: (b, h))
    old = pl.BlockSpec((past, LANE), lambda b, h: (b, h))
    return pl.pallas_call(
        functools.partial(body, past=past),
        out_shape=jax.ShapeDtypeStruct((batch * sq, HEAD_COLS), BF16),
        grid=(batch, H_A),
        in_specs=[new, old, old, new, new] + extra_specs,
        out_specs=new,
        compiler_params=_params(("parallel", "parallel")),
        name=name,
    )(q, kp, vp, kn, vn, *extra)


def _top4_of_row(logits):
    lane = lax.broadcasted_iota(I32, logits.shape, 1)
    vals = logits
    top_v, top_i = [], []
    for _ in range(TOP_K):
        mx = jnp.max(vals, axis=-1, keepdims=True)
        sel = jnp.min(jnp.where(vals == mx, lane, N_EXPERTS), axis=-1, keepdims=True)
        top_v.append(mx)
        top_i.append(sel)
        vals = jnp.where(lane == sel, -jnp.inf, vals)
    e = [jnp.exp(v - top_v[0]) for v in top_v]
    tot = e[0] + e[1] + e[2] + e[3]
    return top_i, [x / tot for x in e]


def _outproj_body(oa_ref, ob_ref, sga_ref, sgb_ref, x_ref, woa_ref, wob_ref, wout_ref, g_ref, b_ref,
                  wr_ref, br_ref, x1_ref, idx_ref, gate_ref):
    a = _dot(oa_ref[...], woa_ref[...])
    b = _dot(ob_ref[...], wob_ref[...])
    merged = sga_ref[...].astype(F32) * a + sgb_ref[...].astype(F32) * b
    y = _dot(merged.astype(BF16), wout_ref[...])
    x1 = _layer_norm(ALPHA * x_ref[...] + y, g_ref[...], b_ref[...])
    x1_ref[...] = x1
    logits = jnp.dot(x1, wr_ref[...], preferred_element_type=F32, precision=lax.Precision.HIGHEST) + br_ref[...]
    top_i, gates = _top4_of_row(logits)
    lane = lax.broadcasted_iota(I32, idx_ref.shape, 1)
    idx = jnp.zeros(idx_ref.shape, I32)
    gate = jnp.zeros(gate_ref.shape, F32)
    for k in range(TOP_K):
        idx = jnp.where(lane == k, top_i[k], idx)
        gate = jnp.where(lane == k, gates[k], gate)
    idx_ref[...] = idx
    gate_ref[...] = gate


def _outproj_call(oa, ob, sga, sgb, x, woa, wob, wout, g, b, wr, br):
    t, d = x.shape
    tm = _pick_tile(t, 256)
    row = lambda n: pl.BlockSpec((tm, n), lambda i: (i, 0))
    const = lambda a: pl.BlockSpec(a.shape, lambda i: (0, 0))
    return pl.pallas_call(
        _outproj_body,
        out_shape=(jax.ShapeDtypeStruct((t, d), F32), jax.ShapeDtypeStruct((t, LANE), I32),
                   jax.ShapeDtypeStruct((t, LANE), F32)),
        grid=(t // tm,),
        in_specs=[row(HEAD_COLS), row(HEAD_COLS), row(HEAD_COLS), row(HEAD_COLS), row(d),
                  const(woa), const(wob), const(wout), const(g), const(b), const(wr), const(br)],
        out_specs=(row(d), row(LANE), row(LANE)),
        compiler_params=_params(("parallel",)),
        name="outproj",
    )(oa, ob, sga, sgb, x, woa, wob, wout, g, b, wr, br)


def _start_row_gather(idx_ref, n, src_hbm, dst, sem):
    def body(r, c):
        pltpu.make_async_copy(src_hbm.at[pl.ds(idx_ref[0, 0, r], 1), :], dst.at[pl.ds(r, 1), :], sem).start()
        return c

    lax.fori_loop(0, n, body, 0)


def _wait_row_gather(n, src_hbm, dst, sem):
    def body(r, c):
        pltpu.make_async_copy(src_hbm.at[pl.ds(0, 1), :], dst.at[pl.ds(r, 1), :], sem).wait()
        return c

    lax.fori_loop(0, n, body, 0)


def _moe_body(be_ref, nb_ref, tok_ref, tokn_ref, x_hbm, wgu_ref, bgu_ref, wdn_ref, bdn_ref, y_ref, xbuf, sem,
              *, bm):
    b = pl.program_id(0)
    nb = nb_ref[0]
    slot = b % 2

    @pl.when(b == 0)
    def _():
        _start_row_gather(tok_ref, bm, x_hbm, xbuf.at[0], sem.at[0])

    @pl.when(b + 1 < nb)
    def _():
        _start_row_gather(tokn_ref, bm, x_hbm, xbuf.at[1 - slot], sem.at[1 - slot])

    @pl.when(b < nb)
    def _():
        _wait_row_gather(bm, x_hbm, xbuf.at[slot], sem.at[slot])
        x = xbuf[slot].astype(BF16)
        h = _dot(x, wgu_ref[0]) + bgu_ref[0]
        g = jnp.minimum(h[:, :D_FF], SWIGLU_LIMIT)
        u = jnp.clip(h[:, D_FF:], -SWIGLU_LIMIT, SWIGLU_LIMIT)
        act = (u + 1.0) * (g * jax.nn.sigmoid(SWIGLU_ALPHA * g))
        y_ref[...] = _dot(act.astype(BF16), wdn_ref[0]) + bdn_ref[0]

    @pl.when(b >= nb)
    def _():
        y_ref[...] = jnp.zeros(y_ref.shape, F32)


def _moe_call(block_e, n_used, row_tok, x, wgu, bgu, wdn, bdn, *, bm):
    n_blocks = block_e.shape[0]
    d = x.shape[1]
    tok3 = row_tok.reshape(n_blocks, 1, bm)
    smem_row = lambda f: pl.BlockSpec((1, 1, bm), f, memory_space=pltpu.SMEM)
    grid_spec = pltpu.PrefetchScalarGridSpec(
        num_scalar_prefetch=2,
        grid=(n_blocks,),
        in_specs=[
            smem_row(lambda b, be, nb: (b, 0, 0)),
            smem_row(lambda b, be, nb: (jnp.minimum(b + 1, n_blocks - 1), 0, 0)),
            pl.BlockSpec(memory_space=pl.ANY),
            pl.BlockSpec((1, d, 2 * D_FF), lambda b, be, nb: (be[b], 0, 0)),
            pl.BlockSpec((1, 1, 2 * D_FF), lambda b, be, nb: (be[b], 0, 0)),
            pl.BlockSpec((1, D_FF, d), lambda b, be, nb: (be[b], 0, 0)),
            pl.BlockSpec((1, 1, d), lambda b, be, nb: (be[b], 0, 0)),
        ],
        out_specs=pl.BlockSpec((bm, d), lambda b, be, nb: (b, 0)),
        scratch_shapes=[pltpu.VMEM((2, bm, d), F32), pltpu.SemaphoreType.DMA((2,))],
    )
    return pl.pallas_call(
        functools.partial(_moe_body, bm=bm),
        out_shape=jax.ShapeDtypeStruct((n_blocks * bm, d), F32),
        grid_spec=grid_spec,
        compiler_params=_params(("arbitrary",)),
        name="moe_experts",
    )(block_e, n_used, tok3, tok3, x, wgu, bgu.reshape(N_EXPERTS, 1, 2 * D_FF), wdn, bdn.reshape(N_EXPERTS, 1, d))


def _combine_body(dest_ref, destn_ref, y_hbm, x_ref, gate_ref, g_ref, b_ref, o_ref, ybuf, sem, *, tc):
    i = pl.program_id(0)
    n = pl.num_programs(0)
    slot = i % 2
    rows = TOP_K * tc

    @pl.when(i == 0)
    def _():
        _start_row_gather(dest_ref, rows, y_hbm, ybuf.at[0], sem.at[0])

    @pl.when(i + 1 < n)
    def _():
        _start_row_gather(destn_ref, rows, y_hbm, ybuf.at[1 - slot], sem.at[1 - slot])

    _wait_row_gather(rows, y_hbm, ybuf.at[slot], sem.at[slot])
    gate = gate_ref[...]
    moe = jnp.zeros(x_ref.shape, F32)
    for k in range(TOP_K):
        moe = moe + gate[:, k:k + 1] * ybuf[slot, k * tc:(k + 1) * tc, :]
    o_ref[...] = _layer_norm(ALPHA * x_ref[...] + moe, g_ref[...], b_ref[...])


def _combine_call(dest, y, x, gate, g, b, *, tc):
    t, d = x.shape
    n = t // tc
    rows = TOP_K * tc
    dest3 = dest.reshape(n, tc, TOP_K).transpose(0, 2, 1).reshape(n, 1, rows)
    smem_row = lambda f: pl.BlockSpec((1, 1, rows), f, memory_space=pltpu.SMEM)
    row = lambda w: pl.BlockSpec((tc, w), lambda i: (i, 0))
    const = lambda a: pl.BlockSpec(a.shape, lambda i: (0, 0))
    return pl.pallas_call(
        functools.partial(_combine_body, tc=tc),
        out_shape=jax.ShapeDtypeStruct((t, d), F32),
        grid=(n,),
        in_specs=[smem_row(lambda i: (i, 0, 0)), smem_row(lambda i: (jnp.minimum(i + 1, n - 1), 0, 0)),
                  pl.BlockSpec(memory_space=pl.ANY), row(d), row(LANE), const(g), const(b)],
        out_specs=row(d),
        scratch_shapes=[pltpu.VMEM((2, rows, d), F32), pltpu.SemaphoreType.DMA((2,))],
        compiler_params=_params(("arbitrary",)),
        name="moe_combine",
    )(dest3, dest3, y, x, gate, g, b)


def _route(top_idx, bm):
    t = top_idx.shape[0]
    n_slots = t * TOP_K
    flat_e = top_idx.reshape(-1)
    order = jnp.argsort(flat_e)
    sorted_e = flat_e[order]
    counts = jnp.bincount(flat_e, length=N_EXPERTS).astype(I32)
    padded = (counts + bm - 1) // bm * bm
    pad_end = jnp.cumsum(padded)
    pad_start = pad_end - padded
    grp_start = jnp.cumsum(counts) - counts
    dest_sorted = pad_start[sorted_e] + jnp.arange(n_slots, dtype=I32) - grp_start[sorted_e]
    n_blocks = -(-n_slots // bm) + N_EXPERTS
    row_tok = jnp.zeros((n_blocks * bm,), I32).at[dest_sorted].set((order // TOP_K).astype(I32))
    dest = jnp.zeros((n_slots,), I32).at[order].set(dest_sorted.astype(I32))
    block_e = jnp.minimum(
        jnp.searchsorted(pad_end, jnp.arange(n_blocks, dtype=I32) * bm, side="right"), N_EXPERTS - 1).astype(I32)
    n_used = (pad_end[-1:] // bm).astype(I32)
    return block_e, n_used, row_tok, dest.reshape(t, TOP_K)


def _head_blocks(w, widths):
    per = sum(widths)
    w = w.reshape(*w.shape[:-1], H_A, per)
    pad = [(0, 0)] * (w.ndim - 1) + [(0, LANE - per)]
    return jnp.pad(w, pad).reshape(*w.shape[:-2], HEAD_COLS)


def _rotate_half_cols(w):
    half = ROPE_DIM // 2
    return jnp.concatenate([-w[..., half:], w[..., :half]], axis=-1)


def _prep_layer(w_in, w_uq, w_ukv, w_oa):
    d = w_in.shape[0]
    splits = (Q_LORA, KV_LORA, ROPE_DIM, 1024, 1024, 1024, D_MODEL, D_MODEL)
    offs = [0]
    for s in splits:
        offs.append(offs[-1] + s)
    lat = w_in[:, offs[0]:offs[2]]
    kr = w_in[:, offs[2]:offs[3]]
    rest = w_in[:, offs[3]:]

    def rope_block(w):
        return jnp.pad(w, ((0, 0), (NOPE_DIM, LANE - NOPE_DIM - ROPE_DIM)))

    w1 = jnp.concatenate([lat, rope_block(kr), rope_block(_rotate_half_cols(kr)), rest], axis=1).astype(BF16)
    assert w1.shape == (d, C_END)

    uq = w_uq.reshape(Q_LORA, H_A, NOPE_DIM + ROPE_DIM)
    uq_rot = jnp.concatenate([jnp.zeros((Q_LORA, H_A, NOPE_DIM), F32), _rotate_half_cols(uq[..., NOPE_DIM:])], -1)
    wuq = jnp.concatenate([_head_blocks(uq.reshape(Q_LORA, -1), (NOPE_DIM + ROPE_DIM,)),
                           _head_blocks(uq_rot.reshape(Q_LORA, -1), (NOPE_DIM + ROPE_DIM,))], axis=1).astype(BF16)

    ukv = w_ukv.reshape(KV_LORA, H_A, NOPE_DIM + V_DIM_A)
    wukv = jnp.concatenate([_head_blocks(ukv[..., :NOPE_DIM].reshape(KV_LORA, -1), (NOPE_DIM,)),
                            _head_blocks(ukv[..., NOPE_DIM:].reshape(KV_LORA, -1), (V_DIM_A,))], axis=1).astype(BF16)

    woa = jnp.pad(w_oa.reshape(H_A, V_DIM_A, D_MODEL), ((0, 0), (0, LANE - V_DIM_A), (0, 0)))
    woa = woa.reshape(HEAD_COLS, D_MODEL).astype(BF16)
    return w1, wuq, wukv, woa


def _rope_tables(pos):
    half = ROPE_DIM // 2
    inv = ROPE_THETA ** (-jnp.arange(half, dtype=F32) / half)
    ang = pos.astype(F32)[:, None] * inv[None, :]
    cos, sin = jnp.cos(ang), jnp.sin(ang)
    n = pos.shape[0]
    tail = jnp.zeros((n, LANE - NOPE_DIM - ROPE_DIM), F32)
    cos_t = jnp.concatenate([jnp.ones((n, NOPE_DIM), F32), cos, cos, tail], axis=1)
    sin_t = jnp.concatenate([jnp.zeros((n, NOPE_DIM), F32), sin, sin, tail], axis=1)
    return cos_t, sin_t


def _rope_block_of(kr):
    return jnp.pad(kr, ((0, 0), (NOPE_DIM, LANE - NOPE_DIM - ROPE_DIM)))


def _moe_layer(x1, top_idx, gate, wgu, bgu, wdn, bdn, g, b):
    t = x1.shape[0]
    bm = 256 if t * TOP_K >= 256 * N_EXPERTS else 128
    block_e, n_used, row_tok, dest = _route(top_idx[:, :TOP_K], bm)
    y = _moe_call(block_e, n_used, row_tok, x1, wgu, bgu, wdn, bdn, bm=bm)
    return _combine_call(dest, y, x1, gate, g, b, tc=_pick_tile(t, 128))


def kernel(x_prompt, x_sample, cache_mla_latent, cache_mla_krope, cache_diff_k, cache_diff_v,
           ln_in_g, ln_in_b, w_in, q_norm_g, kv_norm_g, w_uq, w_ukv, lam_q1, lam_k1, lam_q2, lam_k2,
           subln_g, w_oa, w_ob, w_out, ln1_g, ln1_b, w_router, b_router, w_gu, b_gu, w_dn, b_dn,
           ln2_g, ln2_b):
    bp, sp, d = x_prompt.shape
    bs, ss, _ = x_sample.shape
    past = cache_mla_latent.shape[2]
    assert bp == 1 and d == D_MODEL
    tp, ts = bp * sp, bs * ss

    cos_p, sin_p = _rope_tables(jnp.arange(sp, dtype=I32))
    cos_s, sin_s = _rope_tables(jnp.tile(past + jnp.arange(ss, dtype=I32), bs))
    slopes = 2.0 ** (-8.0 * (jnp.arange(H_B, dtype=F32) + 1.0) / H_B)
    slopes = jnp.broadcast_to(slopes[:, None, None], (H_B, 1, LANE))
    tq = _pick_tile(sp, 256)
    tk = _pick_tile(sp, 512)

    xp = _ln_call(x_prompt.reshape(tp, d), ln_in_g, ln_in_b)
    xs = _ln_call(x_sample.reshape(ts, d), ln_in_g, ln_in_b)
    new_p = ([], [], [], [])
    new_s = ([], [], [], [])
    for l in range(DEPTH):
        lam_init = 0.8 - 0.6 * math.exp(-0.3 * l)
        w1, wuq, wukv, woa = _prep_layer(w_in[l], w_uq[l], w_ukv[l], w_oa[l])
        gq = q_norm_g[l].reshape(1, Q_LORA)
        gkv = kv_norm_g[l].reshape(1, KV_LORA)
        lamv = jnp.stack([lam_q1[l], lam_k1[l], lam_q2[l], lam_k2[l]])
        subg = subln_g[l].reshape(1, 2 * DH_B)
        wob = w_ob[l].astype(BF16)
        wout = w_out[l].astype(BF16)
        g1, b1 = ln1_g[l].reshape(1, d), ln1_b[l].reshape(1, d)
        g2, b2 = ln2_g[l].reshape(1, d), ln2_b[l].reshape(1, d)
        wr, br = w_router[l], b_router[l].reshape(1, N_EXPERTS)
        wgu, wdn = w_gu[l].astype(BF16), w_dn[l].astype(BF16)
        diff_extra = [slopes, lamv, subg]

        def diff_specs(head_of):
            return [pl.BlockSpec((1, 1, LANE), lambda a, b: (head_of(a, b), 0, 0)),
                    pl.BlockSpec(lamv.shape, lambda a, b: (0, 0)), pl.BlockSpec(subg.shape, lambda a, b: (0, 0))]

        qa, lat, krb, qd, kd32, kd16, vd32, vd16, sga, sgb = _inproj_call(xp, w1, wuq, gq, gkv, cos_p, sin_p)
        ka, va = _kvup_call(lat, krb, wukv)
        oa = _prompt_attn_call(_mla_prompt_body, qa, ka, va, [], [], rows_per_q=1, tq=tq, tk=tk, name="mla_prompt")
        ob = _prompt_attn_call(functools.partial(_diff_prompt_body, lam_init=lam_init), qd, kd16, vd16,
                               diff_extra, diff_specs(lambda h, i: h), rows_per_q=2, tq=tq, tk=tk,
                               name="diff_prompt")
        xp1, idx_p, gate_p = _outproj_call(oa, ob, sga, sgb, xp, woa, wob, wout, g1, b1, wr, br)
        new_p[0].append(lat.reshape(bp, sp, KV_LORA))
        new_p[1].append(krb[:, NOPE_DIM:NOPE_DIM + ROPE_DIM].reshape(bp, sp, ROPE_DIM))
        new_p[2].append(kd32.reshape(bp, sp, 2 * H_B, DH_B))
        new_p[3].append(vd32.reshape(bp, sp, H_B, 2 * DH_B))

        qa, lat, krb, qd, kd32, kd16, vd32, vd16, sga, sgb = _inproj_call(xs, w1, wuq, gq, gkv, cos_s, sin_s)
        kn, vn = _kvup_call(lat, krb, wukv)
        kpast, vpast = _kvup_call(cache_mla_latent[l].reshape(bs * past, KV_LORA),
                                  _rope_block_of(cache_mla_krope[l].reshape(bs * past, ROPE_DIM)), wukv)
        oa = _sample_attn_call(_mla_sample_body, qa, kpast, vpast, kn, vn, [], [], batch=bs, past=past,
                               name="mla_sample")
        ob = _sample_attn_call(functools.partial(_diff_sample_body, lam_init=lam_init), qd,
                               cache_diff_k[l].reshape(bs * past, HEAD_COLS),
                               cache_diff_v[l].reshape(bs * past, HEAD_COLS), kd16, vd16,
                               diff_extra, diff_specs(lambda b, h: h), batch=bs, past=past, name="diff_sample")
        xs1, idx_s, gate_s = _outproj_call(oa, ob, sga, sgb, xs, woa, wob, wout, g1, b1, wr, br)
        new_s[0].append(lat.reshape(bs, ss, KV_LORA))
        new_s[1].append(krb[:, NOPE_DIM:NOPE_DIM + ROPE_DIM].reshape(bs, ss, ROPE_DIM))
        new_s[2].append(kd32.reshape(bs, ss, 2 * H_B, DH_B))
        new_s[3].append(vd32.reshape(bs, ss, H_B, 2 * DH_B))

        xp = _moe_layer(xp1, idx_p, gate_p, wgu, b_gu[l], wdn, b_dn[l], g2, b2)
        xs = _moe_layer(xs1, idx_s, gate_s, wgu, b_gu[l], wdn, b_dn[l], g2, b2)

    return (xp.reshape(bp, sp, d), xs.reshape(bs, ss, d),
            jnp.stack(new_p[0]), jnp.stack(new_p[1]), jnp.stack(new_p[2]), jnp.stack(new_p[3]),
            jnp.stack(new_s[0]), jnp.stack(new_s[1]), jnp.stack(new_s[2]), jnp.stack(new_s[3]))
```

```python
import functools
import math

import jax
import jax.numpy as jnp
from jax import lax
from jax.experimental import pallas as pl
from jax.experimental.pallas import tpu as pltpu

F32 = jnp.float32
BF16 = jnp.bfloat16
I32 = jnp.int32

D_MODEL = 1024
DEPTH = 4
CHUNK = 64
CHUNK_SHIFT = 6
H_A = 8
NOPE_DIM = 64
ROPE_DIM = 32
V_DIM_A = 64
Q_LORA = 256
KV_LORA = 256
ROPE_THETA = 10000.0
H_B = 8
DH_B = 64
N_EXPERTS = 32
TOP_K = 4
D_FF = 1024
SWIGLU_LIMIT = 7.0
SWIGLU_ALPHA = 1.702
LN_EPS = 1e-5
RMS_EPS = 1e-6
ALPHA = (2 * DEPTH) ** 0.25

LANE = 128
BF16_ROWS = 16
HEAD_COLS = H_A * LANE
LOG2E = 1.4426950408889634
MLA_QSCALE = (NOPE_DIM + ROPE_DIM) ** -0.5 * LOG2E
DIFF_QSCALE = DH_B ** -0.5 * LOG2E
NEG = -1e30
VMEM_LIMIT = 48 * 1024 * 1024

C_LAT = 0
C_KR = 512
C_KRROT = 640
C_QD = 768
C_KD = C_QD + 1024
C_VD = C_KD + 1024
C_GA = C_VD + 1024
C_GB = C_GA + 1024
C_END = C_GB + 1024


def _pick_tile(n, pref):
    t = min(n, pref)
    while n % t:
        t -= 8
    assert t > 0 and (t % 8 == 0 or t == n), (n, pref)
    return t


def _params(sem):
    return pltpu.CompilerParams(dimension_semantics=sem, vmem_limit_bytes=VMEM_LIMIT)


def _layer_norm(x, g, b):
    mu = jnp.mean(x, axis=-1, keepdims=True)
    xc = x - mu
    var = jnp.mean(xc * xc, axis=-1, keepdims=True)
    return xc * lax.rsqrt(var + LN_EPS) * g + b


def _rms_norm(x, g, eps):
    ms = jnp.mean(x * x, axis=-1, keepdims=True)
    return x * lax.rsqrt(ms + eps) * g


def _dot(a, b):
    return jnp.dot(a, b, preferred_element_type=F32)


def _dot_nt(a, b):
    return lax.dot_general(a, b, (((1,), (1,)), ((), ())), preferred_element_type=F32)


def _ln_body(x_ref, g_ref, b_ref, o_ref):
    o_ref[...] = _layer_norm(x_ref[...], g_ref[...], b_ref[...])


def _ln_call(x, g, b):
    t, d = x.shape
    tm = _pick_tile(t, 512)
    row = pl.BlockSpec((tm, d), lambda i: (i, 0))
    vec = pl.BlockSpec((1, d), lambda i: (0, 0))
    return pl.pallas_call(
        _ln_body,
        out_shape=jax.ShapeDtypeStruct((t, d), F32),
        grid=(t // tm,),
        in_specs=[row, vec, vec],
        out_specs=row,
        compiler_params=_params(("parallel",)),
        name="ln_in",
    )(x, g.reshape(1, d), b.reshape(1, d))


def _inproj_body(x_ref, w_ref, wvt_ref, wuq_ref, gq_ref, gkv_ref, cos_ref, sin_ref,
                 q_ref, lat_ref, krb_ref, qd_ref, kd32_ref, kd16_ref, vd32_ref, vd16_ref, vdt_ref,
                 sga_ref, sgb_ref):
    xb = x_ref[...].astype(BF16)
    cos = cos_ref[...]
    sin = sin_ref[...]

    def proj(lo, hi):
        return _dot(xb, w_ref[:, lo:hi])

    z = proj(C_LAT, C_QD)
    cq = _rms_norm(z[:, :Q_LORA], gq_ref[...], RMS_EPS)
    lat_ref[...] = _rms_norm(z[:, Q_LORA:C_KR], gkv_ref[...], RMS_EPS)
    krb_ref[...] = z[:, C_KR:C_KRROT] * cos + z[:, C_KRROT:C_QD] * sin
    qq = _dot(cq.astype(BF16), wuq_ref[...])
    for h in range(H_A):
        lo = h * LANE
        qh = qq[:, lo:lo + LANE] * cos + qq[:, HEAD_COLS + lo:HEAD_COLS + lo + LANE] * sin
        q_ref[:, lo:lo + LANE] = (qh * MLA_QSCALE).astype(BF16)
    qd_ref[...] = (proj(C_QD, C_KD) * DIFF_QSCALE).astype(BF16)
    z = proj(C_KD, C_VD)
    kd32_ref[...] = z
    kd16_ref[...] = z.astype(BF16)
    z = proj(C_VD, C_GA)
    vd32_ref[...] = z
    vd16_ref[...] = z.astype(BF16)
    vdt_ref[...] = _dot_nt(wvt_ref[...], xb).astype(BF16)
    sga_ref[...] = jax.nn.sigmoid(proj(C_GA, C_GB)).astype(BF16)
    sgb_ref[...] = jax.nn.sigmoid(proj(C_GB, C_END)).astype(BF16)


def _inproj_call(x, w, wvt, wuq, gq, gkv, cos_t, sin_t):
    t, d = x.shape
    tm = _pick_tile(t, 256)
    row = lambda n: pl.BlockSpec((tm, n), lambda i: (i, 0))
    const = lambda a: pl.BlockSpec(a.shape, lambda i: (0, 0), pipeline_mode=pl.Buffered(1))
    wide = lambda dt: jax.ShapeDtypeStruct((t, HEAD_COLS), dt)
    return pl.pallas_call(
        _inproj_body,
        out_shape=(wide(BF16), jax.ShapeDtypeStruct((t, KV_LORA), F32), jax.ShapeDtypeStruct((t, LANE), F32),
                   wide(BF16), wide(F32), wide(BF16), wide(F32), wide(BF16),
                   jax.ShapeDtypeStruct((HEAD_COLS, t), BF16), wide(BF16), wide(BF16)),
        grid=(t // tm,),
        in_specs=[row(d), const(w), const(wvt), const(wuq), const(gq), const(gkv), row(LANE), row(LANE)],
        out_specs=(row(HEAD_COLS), row(KV_LORA), row(LANE), row(HEAD_COLS), row(HEAD_COLS), row(HEAD_COLS),
                   row(HEAD_COLS), row(HEAD_COLS), pl.BlockSpec((HEAD_COLS, tm), lambda i: (0, i)),
                   row(HEAD_COLS), row(HEAD_COLS)),
        compiler_params=_params(("parallel",)),
        name="inproj",
    )(x, w, wvt, wuq, gq, gkv, cos_t, sin_t)


def _kvup_body(lat_ref, krb_ref, wk_ref, wv_ref, k_ref, v_ref, *, feature_major_v):
    lat = lat_ref[...].astype(BF16)
    kn = _dot(lat, wk_ref[...])
    krb = krb_ref[...]
    for h in range(H_A):
        lo = h * LANE
        k_ref[:, lo:lo + LANE] = (kn[:, lo:lo + LANE] + krb).astype(BF16)
    if feature_major_v:
        v_ref[...] = _dot_nt(wv_ref[...], lat).astype(BF16)
    else:
        v_ref[...] = _dot(lat, wv_ref[...]).astype(BF16)


def _kvup_call(lat, krb, wk, wv, *, feature_major_v):
    t = lat.shape[0]
    tm = _pick_tile(t, 512)
    row = lambda n: pl.BlockSpec((tm, n), lambda i: (i, 0))
    const = lambda a: pl.BlockSpec(a.shape, lambda i: (0, 0))
    if feature_major_v:
        v_shape, v_spec = (H_A * V_DIM_A, t), pl.BlockSpec((H_A * V_DIM_A, tm), lambda i: (0, i))
    else:
        v_shape, v_spec = (t, HEAD_COLS), row(HEAD_COLS)
    return pl.pallas_call(
        functools.partial(_kvup_body, feature_major_v=feature_major_v),
        out_shape=(jax.ShapeDtypeStruct((t, HEAD_COLS), BF16), jax.ShapeDtypeStruct(v_shape, BF16)),
        grid=(t // tm,),
        in_specs=[row(KV_LORA), row(LANE), const(wk), const(wv)],
        out_specs=(row(HEAD_COLS), v_spec),
        compiler_params=_params(("parallel",)),
        name="kvup",
    )(lat, krb, wk, wv)


def _stack_diff_queries(q):
    lane = lax.broadcasted_iota(I32, q.shape, 1)
    zero = jnp.zeros_like(q)
    return jnp.concatenate([jnp.where(lane < DH_B, q, zero), jnp.where(lane >= DH_B, q, zero)], axis=0)


def _diff_lambda(lamv_ref, lam_init):
    lv = lamv_ref[...]
    a = jnp.sum(lv[0:1] * lv[1:2], axis=-1, keepdims=True)
    b = jnp.sum(lv[2:3] * lv[3:4], axis=-1, keepdims=True)
    return jnp.exp(a) - jnp.exp(b) + lam_init


def _ones_row_block(tk):
    return (lax.broadcasted_iota(I32, (BF16_ROWS, tk), 0) == 0).astype(BF16)


def _visible_strip(q0, k0, tk):
    c = k0 + lax.broadcasted_iota(I32, (tk, LANE), 0)
    r = q0 + lax.broadcasted_iota(I32, (tk, LANE), 1)
    return (c >> CHUNK_SHIFT) <= (r >> CHUNK_SHIFT)


def _softmax_strips(s_ref, p_ref, m_prev, adjust):
    m_new, alpha = [], []
    for c in range(s_ref.shape[1] // LANE):
        cols = slice(c * LANE, (c + 1) * LANE)
        s = adjust(s_ref[:, cols], c)
        mp = m_prev[:, cols]
        mn = jnp.maximum(mp, jnp.max(s, axis=0, keepdims=True))
        p_ref[:, cols] = jnp.exp2(s - mn).astype(BF16)
        m_new.append(mn)
        alpha.append(jnp.exp2(mp - mn))
    return jnp.concatenate(m_new, axis=1), jnp.concatenate(alpha, axis=1)


def _sweep_key_tiles(scores, values, adjust, s_sc, p_sc, acc_sc, q0, tq, tk):
    width = s_sc.shape[2]
    acc_sc[...] = jnp.zeros(acc_sc.shape, F32)
    p_sc[1] = jnp.zeros(p_sc.shape[1:], BF16)
    n_full = q0 // tk
    s_sc[0] = scores(0)

    def step(j, cur, carry):
        m_prev, alpha_prev = carry
        s_sc[1 - cur] = scores(j + 1)
        acc_sc[...] = alpha_prev * acc_sc[...] + _dot(values(jnp.maximum(j - 1, 0)), p_sc[1 - cur])
        return _softmax_strips(s_sc.at[cur], p_sc.at[cur], m_prev, adjust(j, False))

    carry = (jnp.full((1, width), NEG, F32), jnp.ones((1, width), F32))
    carry = lax.fori_loop(0, n_full // 2, lambda jj, c: step(2 * jj + 1, 1, step(2 * jj, 0, c)), carry)
    m, alpha = lax.fori_loop(0, n_full % 2, lambda _, c: step(n_full - 1, 0, c), carry)
    last = (n_full + 1) % 2
    acc_sc[...] = alpha * acc_sc[...] + _dot(values(jnp.maximum(n_full - 1, 0)), p_sc[last])
    for j in range(max(1, tq // tk)):
        s_sc[0] = scores(n_full + j)
        m, alpha = _softmax_strips(s_sc.at[0], p_sc.at[0], m, adjust(n_full + j, True))
        acc_sc[...] = alpha * acc_sc[...] + _dot(values(n_full + j), p_sc[0])
    return acc_sc[...]


def _mla_prompt_body(q_ref, k_ref, vt_ref, o_ref, s_sc, p_sc, acc_sc, *, tq, tk):
    q0 = pl.program_id(1) * tq
    q = q_ref[...]
    ones = _ones_row_block(tk)

    def key_start(kt):
        return pl.multiple_of(kt * tk, tk)

    def scores(kt):
        return _dot_nt(k_ref[pl.ds(key_start(kt), tk), :], q)

    def values(kt):
        return jnp.concatenate([vt_ref[:, pl.ds(key_start(kt), tk)], ones], axis=0)

    def adjust(kt, masked):
        if not masked:
            return lambda s, c: s
        return lambda s, c: jnp.where(_visible_strip(q0 + c * LANE, kt * tk, tk), s, NEG)

    acc = _sweep_key_tiles(scores, values, adjust, s_sc, p_sc, acc_sc, q0, tq, tk)
    o = acc[:V_DIM_A] / acc[V_DIM_A:V_DIM_A + 1]
    o_ref[:, :V_DIM_A] = o.T.astype(o_ref.dtype)
    o_ref[:, V_DIM_A:] = jnp.zeros((tq, LANE - V_DIM_A), o_ref.dtype)


def _diff_prompt_body(q_ref, k_ref, vt_ref, slope_ref, lamv_ref, g_ref, o_ref, s_sc, p_sc, acc_sc,
                      *, tq, tk, lam_init):
    q0 = pl.program_id(1) * tq
    qs = _stack_diff_queries(q_ref[...])
    slope = slope_ref[0][:, :1] * LOG2E
    ones = _ones_row_block(tk)

    def key_start(kt):
        return pl.multiple_of(kt * tk, tk)

    def scores(kt):
        return _dot_nt(k_ref[pl.ds(key_start(kt), tk), :], qs)

    def values(kt):
        return jnp.concatenate([vt_ref[:, pl.ds(key_start(kt), tk)], ones], axis=0)

    def adjust(kt, masked):
        c_rel = (kt * tk - q0 + lax.broadcasted_iota(I32, (tk, LANE), 0)).astype(F32)
        if not masked:
            bias = slope * c_rel
            return lambda s, c: s + bias

        def fn(s, c):
            off = (c * LANE) % tq
            a = (off + lax.broadcasted_iota(I32, (tk, LANE), 1)).astype(F32)
            bias = slope * jnp.minimum(c_rel, 2.0 * a - c_rel)
            return jnp.where(_visible_strip(q0 + off, kt * tk, tk), s + bias, NEG)

        return fn

    acc = _sweep_key_tiles(scores, values, adjust, s_sc, p_sc, acc_sc, q0, tq, tk)
    dv = 2 * DH_B
    o1 = acc[:dv, :tq] / acc[dv:dv + 1, :tq]
    o2 = acc[:dv, tq:] / acc[dv:dv + 1, tq:]
    o = o1 - _diff_lambda(lamv_ref, lam_init) * o2
    o = o * lax.rsqrt(jnp.mean(o * o, axis=0, keepdims=True) + RMS_EPS)
    o_ref[...] = (o.T * g_ref[...] * (1.0 - lam_init)).astype(o_ref.dtype)


def _prompt_attn_call(body, q, k, vt, extra, extra_specs, *, maps, tq, tk, name):
    t = q.shape[0]
    dv = vt.shape[0] // H_A
    width = maps * tq
    assert t % tq == 0 and t % tk == 0 and (tq % tk == 0 or tk % tq == 0) and tq % LANE == 0 and tk % CHUNK == 0
    qspec = pl.BlockSpec((tq, LANE), lambda h, i: (i, h))
    return pl.pallas_call(
        functools.partial(body, tq=tq, tk=tk),
        out_shape=jax.ShapeDtypeStruct((t, HEAD_COLS), BF16),
        grid=(H_A, t // tq),
        in_specs=[qspec, pl.BlockSpec((t, LANE), lambda h, i: (0, h)),
                  pl.BlockSpec((dv, t), lambda h, i: (h, 0))] + extra_specs,
        out_specs=qspec,
        scratch_shapes=[pltpu.VMEM((2, tk, width), F32), pltpu.VMEM((2, tk, width), BF16),
                        pltpu.VMEM((dv + BF16_ROWS, width), F32)],
        compiler_params=_params(("parallel", "arbitrary")),
        name=name,
    )(q, k, vt, *extra)


def _chunk_visible(q0, k0, shape):
    r = q0 + lax.broadcasted_iota(I32, shape, 0)
    c = k0 + lax.broadcasted_iota(I32, shape, 1)
    return (c >> CHUNK_SHIFT) <= (r >> CHUNK_SHIFT)


def _two_part_softmax_pv(s_past, s_new, v_past, v_new):
    m = jnp.maximum(jnp.max(s_past, axis=-1, keepdims=True), jnp.max(s_new, axis=-1, keepdims=True))
    p_past = jnp.exp2(s_past - m)
    p_new = jnp.exp2(s_new - m)
    l = jnp.sum(p_past, axis=-1, keepdims=True) + jnp.sum(p_new, axis=-1, keepdims=True)
    acc = _dot(p_past.astype(BF16), v_past) + _dot(p_new.astype(BF16), v_new)
    return acc / l


def _mla_sample_body(q_ref, kp_ref, vp_ref, kn_ref, vn_ref, o_ref, *, past):
    q = q_ref[...]
    s_past = _dot_nt(q, kp_ref[...])
    s_new = _dot_nt(q, kn_ref[...])
    s_past = jnp.where(_chunk_visible(past, 0, s_past.shape), s_past, NEG)
    s_new = jnp.where(_chunk_visible(past, past, s_new.shape), s_new, NEG)
    o_ref[...] = _two_part_softmax_pv(s_past, s_new, vp_ref[...], vn_ref[...]).astype(o_ref.dtype)


def _diff_sample_body(q_ref, kp_ref, vp_ref, kn_ref, vn_ref, slope_ref, lamv_ref, g_ref, o_ref,
                      *, past, lam_init):
    sq = q_ref.shape[0]
    qs = _stack_diff_queries(q_ref[...])
    slope = slope_ref[0][:, :1] * LOG2E

    def scores(k, k0):
        n = k.shape[0]
        s = _dot_nt(qs, k.astype(BF16))
        qpos = past + lax.broadcasted_iota(I32, (sq, n), 0)
        kpos = k0 + lax.broadcasted_iota(I32, (sq, n), 1)
        bias = -slope * jnp.abs(qpos - kpos).astype(F32)
        vis = _chunk_visible(past, k0, (sq, n))
        bias = jnp.concatenate([bias, bias], axis=0)
        vis = jnp.concatenate([vis, vis], axis=0)
        return jnp.where(vis, s + bias, NEG)

    o = _two_part_softmax_pv(scores(kp_ref[...], 0), scores(kn_ref[...], past),
                             vp_ref[...].astype(BF16), vn_ref[...])
    lam = _diff_lambda(lamv_ref, lam_init)
    o = o[:sq] - lam * o[sq:]
    o_ref[...] = (_rms_norm(o, g_ref[...], RMS_EPS) * (1.0 - lam_init)).astype(o_ref.dtype)


def _sample_attn_call(body, q, kp, vp, kn, vn, extra, extra_specs, *, batch, past, name):
    sq = q.shape[0] // batch
    assert past % CHUNK == 0 and kp.shape[0] == batch * past
    new = pl.BlockSpec((sq, LANE), lambda b, h: (b, h))
    old = pl.BlockSpec((past, LANE), lambda b, h: (b, h))
    return pl.pallas_call(
        functools.partial(body, past=past),
        out_shape=jax.ShapeDtypeStruct((batch * sq, HEAD_COLS), BF16),
        grid=(batch, H_A),
        in_specs=[new, old, old, new, new] + extra_specs,
        out_specs=new,
        compiler_params=_params(("parallel", "parallel")),
        name=name,
    )(q, kp, vp, kn, vn, *extra)


def _top4_of_row(logits):
    lane = lax.broadcasted_iota(I32, logits.shape, 1)
    vals = logits
    top_v, top_i = [], []
    for _ in range(TOP_K):
        mx = jnp.max(vals, axis=-1, keepdims=True)
        sel = jnp.min(jnp.where(vals == mx, lane, N_EXPERTS), axis=-1, keepdims=True)
        top_v.append(mx)
        top_i.append(sel)
        vals = jnp.where(lane == sel, -jnp.inf, vals)
    e = [jnp.exp(v - top_v[0]) for v in top_v]
    tot = e[0] + e[1] + e[2] + e[3]
    return top_i, [x / tot for x in e]


def _outproj_body(oa_ref, ob_ref, sga_ref, sgb_ref, x_ref, woa_ref, wob_ref, wout_ref, g_ref, b_ref,
                  wr_ref, br_ref, x1_ref, idx_ref, gate_ref):
    a = _dot(oa_ref[...], woa_ref[...])
    b = _dot(ob_ref[...], wob_ref[...])
    merged = sga_ref[...].astype(F32) * a + sgb_ref[...].astype(F32) * b
    y = _dot(merged.astype(BF16), wout_ref[...])
    x1 = _layer_norm(ALPHA * x_ref[...] + y, g_ref[...], b_ref[...])
    x1_ref[...] = x1
    logits = jnp.dot(x1, wr_ref[...], preferred_element_type=F32, precision=lax.Precision.HIGHEST) + br_ref[...]
    top_i, gates = _top4_of_row(logits)
    lane = lax.broadcasted_iota(I32, idx_ref.shape, 1)
    idx = jnp.zeros(idx_ref.shape, I32)
    gate = jnp.zeros(gate_ref.shape, F32)
    for k in range(TOP_K):
        idx = jnp.where(lane == k, top_i[k], idx)
        gate = jnp.where(lane == k, gates[k], gate)
    idx_ref[...] = idx
    gate_ref[...] = gate


def _outproj_call(oa, ob, sga, sgb, x, woa, wob, wout, g, b, wr, br):
    t, d = x.shape
    tm = _pick_tile(t, 256)
    row = lambda n: pl.BlockSpec((tm, n), lambda i: (i, 0))
    const = lambda a: pl.BlockSpec(a.shape, lambda i: (0, 0))
    return pl.pallas_call(
        _outproj_body,
        out_shape=(jax.ShapeDtypeStruct((t, d), F32), jax.ShapeDtypeStruct((t, LANE), I32),
                   jax.ShapeDtypeStruct((t, LANE), F32)),
        grid=(t // tm,),
        in_specs=[row(HEAD_COLS), row(HEAD_COLS), row(HEAD_COLS), row(HEAD_COLS), row(d),
                  const(woa), const(wob), const(wout), const(g), const(b), const(wr), const(br)],
        out_specs=(row(d), row(LANE), row(LANE)),
        compiler_params=_params(("parallel",)),
        name="outproj",
    )(oa, ob, sga, sgb, x, woa, wob, wout, g, b, wr, br)


def _start_row_gather(idx_ref, n, src_hbm, dst, sem):
    def body(r, c):
        pltpu.make_async_copy(src_hbm.at[pl.ds(idx_ref[0, 0, r], 1), :], dst.at[pl.ds(r, 1), :], sem).start()
        return c

    lax.fori_loop(0, n, body, 0)


def _wait_row_gather(n, src_hbm, dst, sem):
    def body(r, c):
        pltpu.make_async_copy(src_hbm.at[pl.ds(0, 1), :], dst.at[pl.ds(r, 1), :], sem).wait()
        return c

    lax.fori_loop(0, n, body, 0)


def _moe_body(be_ref, nb_ref, tok_ref, tokn_ref, x_hbm, wgu_ref, bgu_ref, wdn_ref, bdn_ref, y_ref, xbuf, sem,
              *, bm):
    b = pl.program_id(0)
    nb = nb_ref[0]
    slot = b % 2

    @pl.when(b == 0)
    def _():
        _start_row_gather(tok_ref, bm, x_hbm, xbuf.at[0], sem.at[0])

    @pl.when(b + 1 < nb)
    def _():
        _start_row_gather(tokn_ref, bm, x_hbm, xbuf.at[1 - slot], sem.at[1 - slot])

    @pl.when(b < nb)
    def _():
        _wait_row_gather(bm, x_hbm, xbuf.at[slot], sem.at[slot])
        x = xbuf[slot].astype(BF16)
        h = _dot(x, wgu_ref[0]) + bgu_ref[0]
        g = jnp.minimum(h[:, :D_FF], SWIGLU_LIMIT)
        u = jnp.clip(h[:, D_FF:], -SWIGLU_LIMIT, SWIGLU_LIMIT)
        act = (u + 1.0) * (g * jax.nn.sigmoid(SWIGLU_ALPHA * g))
        y_ref[...] = _dot(act.astype(BF16), wdn_ref[0]) + bdn_ref[0]

    @pl.when(b >= nb)
    def _():
        y_ref[...] = jnp.zeros(y_ref.shape, F32)


def _moe_call(block_e, n_used, row_tok, x, wgu, bgu, wdn, bdn, *, bm):
    n_blocks = block_e.shape[0]
    d = x.shape[1]
    tok3 = row_tok.reshape(n_blocks, 1, bm)
    smem_row = lambda f: pl.BlockSpec((1, 1, bm), f, memory_space=pltpu.SMEM)
    grid_spec = pltpu.PrefetchScalarGridSpec(
        num_scalar_prefetch=2,
        grid=(n_blocks,),
        in_specs=[
            smem_row(lambda b, be, nb: (b, 0, 0)),
            smem_row(lambda b, be, nb: (jnp.minimum(b + 1, n_blocks - 1), 0, 0)),
            pl.BlockSpec(memory_space=pl.ANY),
            pl.BlockSpec((1, d, 2 * D_FF), lambda b, be, nb: (be[b], 0, 0)),
            pl.BlockSpec((1, 1, 2 * D_FF), lambda b, be, nb: (be[b], 0, 0)),
            pl.BlockSpec((1, D_FF, d), lambda b, be, nb: (be[b], 0, 0)),
            pl.BlockSpec((1, 1, d), lambda b, be, nb: (be[b], 0, 0)),
        ],
        out_specs=pl.BlockSpec((bm, d), lambda b, be, nb: (b, 0)),
        scratch_shapes=[pltpu.VMEM((2, bm, d), F32), pltpu.SemaphoreType.DMA((2,))],
    )
    return pl.pallas_call(
        functools.partial(_moe_body, bm=bm),
        out_shape=jax.ShapeDtypeStruct((n_blocks * bm, d), F32),
        grid_spec=grid_spec,
        compiler_params=_params(("arbitrary",)),
        name="moe_experts",
    )(block_e, n_used, tok3, tok3, x, wgu, bgu.reshape(N_EXPERTS, 1, 2 * D_FF), wdn, bdn.reshape(N_EXPERTS, 1, d))


def _combine_body(dest_ref, destn_ref, y_hbm, x_ref, gate_ref, g_ref, b_ref, o_ref, ybuf, sem, *, tc):
    i = pl.program_id(0)
    n = pl.num_programs(0)
    slot = i % 2
    rows = TOP_K * tc

    @pl.when(i == 0)
    def _():
        _start_row_gather(dest_ref, rows, y_hbm, ybuf.at[0], sem.at[0])

    @pl.when(i + 1 < n)
    def _():
        _start_row_gather(destn_ref, rows, y_hbm, ybuf.at[1 - slot], sem.at[1 - slot])

    _wait_row_gather(rows, y_hbm, ybuf.at[slot], sem.at[slot])
    gate = gate_ref[...]
    moe = jnp.zeros(x_ref.shape, F32)
    for k in range(TOP_K):
        moe = moe + gate[:, k:k + 1] * ybuf[slot, k * tc:(k + 1) * tc, :]
    o_ref[...] = _layer_norm(ALPHA * x_ref[...] + moe, g_ref[...], b_ref[...])


def _combine_call(dest, y, x, gate, g, b, *, tc):
    t, d = x.shape
    n = t // tc
    rows = TOP_K * tc
    dest3 = dest.reshape(n, tc, TOP_K).transpose(0, 2, 1).reshape(n, 1, rows)
    smem_row = lambda f: pl.BlockSpec((1, 1, rows), f, memory_space=pltpu.SMEM)
    row = lambda w: pl.BlockSpec((tc, w), lambda i: (i, 0))
    const = lambda a: pl.BlockSpec(a.shape, lambda i: (0, 0))
    return pl.pallas_call(
        functools.partial(_combine_body, tc=tc),
        out_shape=jax.ShapeDtypeStruct((t, d), F32),
        grid=(n,),
        in_specs=[smem_row(lambda i: (i, 0, 0)), smem_row(lambda i: (jnp.minimum(i + 1, n - 1), 0, 0)),
                  pl.BlockSpec(memory_space=pl.ANY), row(d), row(LANE), const(g), const(b)],
        out_specs=row(d),
        scratch_shapes=[pltpu.VMEM((2, rows, d), F32), pltpu.SemaphoreType.DMA((2,))],
        compiler_params=_params(("arbitrary",)),
        name="moe_combine",
    )(dest3, dest3, y, x, gate, g, b)


def _route(top_idx, bm):
    t = top_idx.shape[0]
    n_slots = t * TOP_K
    flat_e = top_idx.reshape(-1)
    order = jnp.argsort(flat_e)
    sorted_e = flat_e[order]
    counts = jnp.bincount(flat_e, length=N_EXPERTS).astype(I32)
    padded = (counts + bm - 1) // bm * bm
    pad_end = jnp.cumsum(padded)
    pad_start = pad_end - padded
    grp_start = jnp.cumsum(counts) - counts
    dest_sorted = pad_start[sorted_e] + jnp.arange(n_slots, dtype=I32) - grp_start[sorted_e]
    n_blocks = -(-n_slots // bm) + N_EXPERTS
    row_tok = jnp.zeros((n_blocks * bm,), I32).at[dest_sorted].set((order // TOP_K).astype(I32))
    dest = jnp.zeros((n_slots,), I32).at[order].set(dest_sorted.astype(I32))
    block_e = jnp.minimum(
        jnp.searchsorted(pad_end, jnp.arange(n_blocks, dtype=I32) * bm, side="right"), N_EXPERTS - 1).astype(I32)
    n_used = (pad_end[-1:] // bm).astype(I32)
    return block_e, n_used, row_tok, dest.reshape(t, TOP_K)


def _head_blocks(w, widths):
    per = sum(widths)
    w = w.reshape(*w.shape[:-1], H_A, per)
    pad = [(0, 0)] * (w.ndim - 1) + [(0, LANE - per)]
    return jnp.pad(w, pad).reshape(*w.shape[:-2], HEAD_COLS)


def _rotate_half_cols(w):
    half = ROPE_DIM // 2
    return jnp.concatenate([-w[..., half:], w[..., :half]], axis=-1)


def _prep_layer(w_in, w_uq, w_ukv, w_oa):
    d = w_in.shape[0]
    splits = (Q_LORA, KV_LORA, ROPE_DIM, 1024, 1024, 1024, D_MODEL, D_MODEL)
    offs = [0]
    for s in splits:
        offs.append(offs[-1] + s)
    lat = w_in[:, offs[0]:offs[2]]
    kr = w_in[:, offs[2]:offs[3]]
    rest = w_in[:, offs[3]:]

    def rope_block(w):
        return jnp.pad(w, ((0, 0), (NOPE_DIM, LANE - NOPE_DIM - ROPE_DIM)))

    w1 = jnp.concatenate([lat, rope_block(kr), rope_block(_rotate_half_cols(kr)), rest], axis=1).astype(BF16)
    assert w1.shape == (d, C_END)
    wvt = w_in[:, offs[5]:offs[6]].T.astype(BF16)

    uq = w_uq.reshape(Q_LORA, H_A, NOPE_DIM + ROPE_DIM)
    uq_rot = jnp.concatenate([jnp.zeros((Q_LORA, H_A, NOPE_DIM), F32), _rotate_half_cols(uq[..., NOPE_DIM:])], -1)
    wuq = jnp.concatenate([_head_blocks(uq.reshape(Q_LORA, -1), (NOPE_DIM + ROPE_DIM,)),
                           _head_blocks(uq_rot.reshape(Q_LORA, -1), (NOPE_DIM + ROPE_DIM,))], axis=1).astype(BF16)

    ukv = w_ukv.reshape(KV_LORA, H_A, NOPE_DIM + V_DIM_A)
    wuk = _head_blocks(ukv[..., :NOPE_DIM].reshape(KV_LORA, -1), (NOPE_DIM,)).astype(BF16)
    wuv = _head_blocks(ukv[..., NOPE_DIM:].reshape(KV_LORA, -1), (V_DIM_A,)).astype(BF16)
    wuvt = ukv[..., NOPE_DIM:].reshape(KV_LORA, H_A * V_DIM_A).T.astype(BF16)

    woa = jnp.pad(w_oa.reshape(H_A, V_DIM_A, D_MODEL), ((0, 0), (0, LANE - V_DIM_A), (0, 0)))
    woa = woa.reshape(HEAD_COLS, D_MODEL).astype(BF16)
    return w1, wvt, wuq, wuk, wuv, wuvt, woa


def _rope_tables(pos):
    half = ROPE_DIM // 2
    inv = ROPE_THETA ** (-jnp.arange(half, dtype=F32) / half)
    ang = pos.astype(F32)[:, None] * inv[None, :]
    cos, sin = jnp.cos(ang), jnp.sin(ang)
    n = pos.shape[0]
    tail = jnp.zeros((n, LANE - NOPE_DIM - ROPE_DIM), F32)
    cos_t = jnp.concatenate([jnp.ones((n, NOPE_DIM), F32), cos, cos, tail], axis=1)
    sin_t = jnp.concatenate([jnp.zeros((n, NOPE_DIM), F32), sin, sin, tail], axis=1)
    return cos_t, sin_t


def _rope_block_of(kr):
    return jnp.pad(kr, ((0, 0), (NOPE_DIM, LANE - NOPE_DIM - ROPE_DIM)))


def _moe_layer(x1, top_idx, gate, wgu, bgu, wdn, bdn, g, b):
    t = x1.shape[0]
    bm = 256 if t * TOP_K >= 256 * N_EXPERTS else 128
    block_e, n_used, row_tok, dest = _route(top_idx[:, :TOP_K], bm)
    y = _moe_call(block_e, n_used, row_tok, x1, wgu, bgu, wdn, bdn, bm=bm)
    return _combine_call(dest, y, x1, gate, g, b, tc=_pick_tile(t, 128))


def kernel(x_prompt, x_sample, cache_mla_latent, cache_mla_krope, cache_diff_k, cache_diff_v,
           ln_in_g, ln_in_b, w_in, q_norm_g, kv_norm_g, w_uq, w_ukv, lam_q1, lam_k1, lam_q2, lam_k2,
           subln_g, w_oa, w_ob, w_out, ln1_g, ln1_b, w_router, b_router, w_gu, b_gu, w_dn, b_dn,
           ln2_g, ln2_b):
    bp, sp, d = x_prompt.shape
    bs, ss, _ = x_sample.shape
    past = cache_mla_latent.shape[2]
    assert bp == 1 and d == D_MODEL
    tp, ts = bp * sp, bs * ss

    cos_p, sin_p = _rope_tables(jnp.arange(sp, dtype=I32))
    cos_s, sin_s = _rope_tables(jnp.tile(past + jnp.arange(ss, dtype=I32), bs))
    slopes = 2.0 ** (-8.0 * (jnp.arange(H_B, dtype=F32) + 1.0) / H_B)
    slopes = jnp.broadcast_to(slopes[:, None, None], (H_B, 1, LANE))
    tk = _pick_tile(sp, 256)
    tq_mla = _pick_tile(sp, 512)
    tq_diff = _pick_tile(sp, 256)

    xp = _ln_call(x_prompt.reshape(tp, d), ln_in_g, ln_in_b)
    xs = _ln_call(x_sample.reshape(ts, d), ln_in_g, ln_in_b)
    new_p = ([], [], [], [])
    new_s = ([], [], [], [])
    for l in range(DEPTH):
        lam_init = 0.8 - 0.6 * math.exp(-0.3 * l)
        w1, wvt, wuq, wuk, wuv, wuvt, woa = _prep_layer(w_in[l], w_uq[l], w_ukv[l], w_oa[l])
        gq = q_norm_g[l].reshape(1, Q_LORA)
        gkv = kv_norm_g[l].reshape(1, KV_LORA)
        lamv = jnp.stack([lam_q1[l], lam_k1[l], lam_q2[l], lam_k2[l]])
        subg = subln_g[l].reshape(1, 2 * DH_B)
        wob = w_ob[l].astype(BF16)
        wout = w_out[l].astype(BF16)
        g1, b1 = ln1_g[l].reshape(1, d), ln1_b[l].reshape(1, d)
        g2, b2 = ln2_g[l].reshape(1, d), ln2_b[l].reshape(1, d)
        wr, br = w_router[l], b_router[l].reshape(1, N_EXPERTS)
        wgu, wdn = w_gu[l].astype(BF16), w_dn[l].astype(BF16)
        diff_extra = [slopes, lamv, subg]

        def diff_specs(head_of):
            return [pl.BlockSpec((1, 1, LANE), lambda a, b: (head_of(a, b), 0, 0)),
                    pl.BlockSpec(lamv.shape, lambda a, b: (0, 0)), pl.BlockSpec(subg.shape, lambda a, b: (0, 0))]

        qa, lat, krb, qd, kd32, kd16, vd32, _, vdt, sga, sgb = _inproj_call(xp, w1, wvt, wuq, gq, gkv, cos_p, sin_p)
        ka, vat = _kvup_call(lat, krb, wuk, wuvt, feature_major_v=True)
        oa = _prompt_attn_call(_mla_prompt_body, qa, ka, vat, [], [], maps=1, tq=tq_mla, tk=tk,
                               name="mla_prompt")
        ob = _prompt_attn_call(functools.partial(_diff_prompt_body, lam_init=lam_init), qd, kd16, vdt,
                               diff_extra, diff_specs(lambda h, i: h), maps=2, tq=tq_diff, tk=tk,
                               name="diff_prompt")
        xp1, idx_p, gate_p = _outproj_call(oa, ob, sga, sgb, xp, woa, wob, wout, g1, b1, wr, br)
        new_p[0].append(lat.reshape(bp, sp, KV_LORA))
        new_p[1].append(krb[:, NOPE_DIM:NOPE_DIM + ROPE_DIM].reshape(bp, sp, ROPE_DIM))
        new_p[2].append(kd32.reshape(bp, sp, 2 * H_B, DH_B))
        new_p[3].append(vd32.reshape(bp, sp, H_B, 2 * DH_B))

        qa, lat, krb, qd, kd32, kd16, vd32, vd16, _, sga, sgb = _inproj_call(xs, w1, wvt, wuq, gq, gkv, cos_s, sin_s)
        kn, vn = _kvup_call(lat, krb, wuk, wuv, feature_major_v=False)
        kpast, vpast = _kvup_call(cache_mla_latent[l].reshape(bs * past, KV_LORA),
                                  _rope_block_of(cache_mla_krope[l].reshape(bs * past, ROPE_DIM)), wuk, wuv,
                                  feature_major_v=False)
        oa = _sample_attn_call(_mla_sample_body, qa, kpast, vpast, kn, vn, [], [], batch=bs, past=past,
                               name="mla_sample")
        ob = _sample_attn_call(functools.partial(_diff_sample_body, lam_init=lam_init), qd,
                               cache_diff_k[l].reshape(bs * past, HEAD_COLS),
                               cache_diff_v[l].reshape(bs * past, HEAD_COLS), kd16, vd16,
                               diff_extra, diff_specs(lambda b, h: h), batch=bs, past=past, name="diff_sample")
        xs1, idx_s, gate_s = _outproj_call(oa, ob, sga, sgb, xs, woa, wob, wout, g1, b1, wr, br)
        new_s[0].append(lat.reshape(bs, ss, KV_LORA))
        new_s[1].append(krb[:, NOPE_DIM:NOPE_DIM + ROPE_DIM].reshape(bs, ss, ROPE_DIM))
        new_s[2].append(kd32.reshape(bs, ss, 2 * H_B, DH_B))
        new_s[3].append(vd32.reshape(bs, ss, H_B, 2 * DH_B))

        xp = _moe_layer(xp1, idx_p, gate_p, wgu, b_gu[l], wdn, b_dn[l], g2, b2)
        xs = _moe_layer(xs1, idx_s, gate_s, wgu, b_gu[l], wdn, b_dn[l], g2, b2)

    return (xp.reshape(bp, sp, d), xs.reshape(bs, ss, d),
            jnp.stack(new_p[0]), jnp.stack(new_p[1]), jnp.stack(new_p[2]), jnp.stack(new_p[3]),
            jnp.stack(new_s[0]), jnp.stack(new_s[1]), jnp.stack(new_s[2]), jnp.stack(new_s[3]))
```

```python
import functools
import math

import jax
import jax.numpy as jnp
from jax import lax
from jax.experimental import pallas as pl
from jax.experimental.pallas import tpu as pltpu

F32 = jnp.float32
BF16 = jnp.bfloat16
I32 = jnp.int32

D_MODEL = 1024
DEPTH = 4
CHUNK = 64
CHUNK_SHIFT = 6
H_A = 8
NOPE_DIM = 64
ROPE_DIM = 32
V_DIM_A = 64
Q_LORA = 256
KV_LORA = 256
ROPE_THETA = 10000.0
H_B = 8
DH_B = 64
N_EXPERTS = 32
TOP_K = 4
D_FF = 1024
SWIGLU_LIMIT = 7.0
SWIGLU_ALPHA = 1.702
LN_EPS = 1e-5
RMS_EPS = 1e-6
ALPHA = (2 * DEPTH) ** 0.25

LANE = 128
BF16_ROWS = 16
HEAD_COLS = H_A * LANE
LOG2E = 1.4426950408889634
MLA_QSCALE = (NOPE_DIM + ROPE_DIM) ** -0.5 * LOG2E
DIFF_QSCALE = DH_B ** -0.5 * LOG2E
MLA_UNROLL = 8
DIFF_UNROLL = 4
NEG = -1e30
VMEM_LIMIT = 48 * 1024 * 1024

C_LAT = 0
C_KR = 512
C_KRROT = 640
C_QD = 768
C_KD = C_QD + 1024
C_VD = C_KD + 1024
C_GA = C_VD + 1024
C_GB = C_GA + 1024
C_END = C_GB + 1024


def _pick_tile(n, pref):
    t = min(n, pref)
    while n % t:
        t -= 8
    assert t > 0 and (t % 8 == 0 or t == n), (n, pref)
    return t


def _params(sem):
    return pltpu.CompilerParams(dimension_semantics=sem, vmem_limit_bytes=VMEM_LIMIT)


def _layer_norm(x, g, b):
    mu = jnp.mean(x, axis=-1, keepdims=True)
    xc = x - mu
    var = jnp.mean(xc * xc, axis=-1, keepdims=True)
    return xc * lax.rsqrt(var + LN_EPS) * g + b


def _rms_norm(x, g, eps):
    ms = jnp.mean(x * x, axis=-1, keepdims=True)
    return x * lax.rsqrt(ms + eps) * g


def _dot(a, b):
    return jnp.dot(a, b, preferred_element_type=F32)


def _dot_nt(a, b):
    return lax.dot_general(a, b, (((1,), (1,)), ((), ())), preferred_element_type=F32)


def _ln_body(x_ref, g_ref, b_ref, o_ref):
    o_ref[...] = _layer_norm(x_ref[...], g_ref[...], b_ref[...])


def _ln_call(x, g, b):
    t, d = x.shape
    tm = _pick_tile(t, 512)
    row = pl.BlockSpec((tm, d), lambda i: (i, 0))
    vec = pl.BlockSpec((1, d), lambda i: (0, 0))
    return pl.pallas_call(
        _ln_body,
        out_shape=jax.ShapeDtypeStruct((t, d), F32),
        grid=(t // tm,),
        in_specs=[row, vec, vec],
        out_specs=row,
        compiler_params=_params(("parallel",)),
        name="ln_in",
    )(x, g.reshape(1, d), b.reshape(1, d))


def _inproj_body(x_ref, w_ref, wvt_ref, wuq_ref, gq_ref, gkv_ref, cos_ref, sin_ref,
                 q_ref, lat_ref, krb_ref, qd_ref, kd32_ref, kd16_ref, vd32_ref, vd16_ref, vdt_ref,
                 sga_ref, sgb_ref):
    xb = x_ref[...].astype(BF16)
    cos = cos_ref[...]
    sin = sin_ref[...]

    def proj(lo, hi):
        return _dot(xb, w_ref[:, lo:hi])

    z = proj(C_LAT, C_QD)
    cq = _rms_norm(z[:, :Q_LORA], gq_ref[...], RMS_EPS)
    lat_ref[...] = _rms_norm(z[:, Q_LORA:C_KR], gkv_ref[...], RMS_EPS)
    krb_ref[...] = z[:, C_KR:C_KRROT] * cos + z[:, C_KRROT:C_QD] * sin
    qq = _dot(cq.astype(BF16), wuq_ref[...])
    for h in range(H_A):
        lo = h * LANE
        qh = qq[:, lo:lo + LANE] * cos + qq[:, HEAD_COLS + lo:HEAD_COLS + lo + LANE] * sin
        q_ref[:, lo:lo + LANE] = (qh * MLA_QSCALE).astype(BF16)
    qd_ref[...] = (proj(C_QD, C_KD) * DIFF_QSCALE).astype(BF16)
    z = proj(C_KD, C_VD)
    kd32_ref[...] = z
    kd16_ref[...] = z.astype(BF16)
    z = proj(C_VD, C_GA)
    vd32_ref[...] = z
    vd16_ref[...] = z.astype(BF16)
    vdt_ref[...] = _dot_nt(wvt_ref[...], xb).astype(BF16)
    sga_ref[...] = jax.nn.sigmoid(proj(C_GA, C_GB)).astype(BF16)
    sgb_ref[...] = jax.nn.sigmoid(proj(C_GB, C_END)).astype(BF16)


def _inproj_call(x, w, wvt, wuq, gq, gkv, cos_t, sin_t):
    t, d = x.shape
    tm = _pick_tile(t, 256)
    row = lambda n: pl.BlockSpec((tm, n), lambda i: (i, 0))
    const = lambda a: pl.BlockSpec(a.shape, lambda i: (0, 0), pipeline_mode=pl.Buffered(1))
    wide = lambda dt: jax.ShapeDtypeStruct((t, HEAD_COLS), dt)
    return pl.pallas_call(
        _inproj_body,
        out_shape=(wide(BF16), jax.ShapeDtypeStruct((t, KV_LORA), F32), jax.ShapeDtypeStruct((t, LANE), F32),
                   wide(BF16), wide(F32), wide(BF16), wide(F32), wide(BF16),
                   jax.ShapeDtypeStruct((HEAD_COLS, t), BF16), wide(BF16), wide(BF16)),
        grid=(t // tm,),
        in_specs=[row(d), const(w), const(wvt), const(wuq), const(gq), const(gkv), row(LANE), row(LANE)],
        out_specs=(row(HEAD_COLS), row(KV_LORA), row(LANE), row(HEAD_COLS), row(HEAD_COLS), row(HEAD_COLS),
                   row(HEAD_COLS), row(HEAD_COLS), pl.BlockSpec((HEAD_COLS, tm), lambda i: (0, i)),
                   row(HEAD_COLS), row(HEAD_COLS)),
        compiler_params=_params(("parallel",)),
        name="inproj",
    )(x, w, wvt, wuq, gq, gkv, cos_t, sin_t)


def _kvup_body(lat_ref, krb_ref, wk_ref, wv_ref, k_ref, v_ref, *, feature_major_v):
    lat = lat_ref[...].astype(BF16)
    kn = _dot(lat, wk_ref[...])
    krb = krb_ref[...]
    for h in range(H_A):
        lo = h * LANE
        k_ref[:, lo:lo + LANE] = (kn[:, lo:lo + LANE] + krb).astype(BF16)
    if feature_major_v:
        v_ref[...] = _dot_nt(wv_ref[...], lat).astype(BF16)
    else:
        v_ref[...] = _dot(lat, wv_ref[...]).astype(BF16)


def _kvup_call(lat, krb, wk, wv, *, feature_major_v):
    t = lat.shape[0]
    tm = _pick_tile(t, 512)
    row = lambda n: pl.BlockSpec((tm, n), lambda i: (i, 0))
    const = lambda a: pl.BlockSpec(a.shape, lambda i: (0, 0))
    if feature_major_v:
        v_shape, v_spec = (H_A * V_DIM_A, t), pl.BlockSpec((H_A * V_DIM_A, tm), lambda i: (0, i))
    else:
        v_shape, v_spec = (t, HEAD_COLS), row(HEAD_COLS)
    return pl.pallas_call(
        functools.partial(_kvup_body, feature_major_v=feature_major_v),
        out_shape=(jax.ShapeDtypeStruct((t, HEAD_COLS), BF16), jax.ShapeDtypeStruct(v_shape, BF16)),
        grid=(t // tm,),
        in_specs=[row(KV_LORA), row(LANE), const(wk), const(wv)],
        out_specs=(row(HEAD_COLS), v_spec),
        compiler_params=_params(("parallel",)),
        name="kvup",
    )(lat, krb, wk, wv)


def _stack_diff_queries(q):
    lane = lax.broadcasted_iota(I32, q.shape, 1)
    zero = jnp.zeros_like(q)
    return jnp.concatenate([jnp.where(lane < DH_B, q, zero), jnp.where(lane >= DH_B, q, zero)], axis=0)


def _diff_lambda(lamv_ref, lam_init):
    lv = lamv_ref[...]
    a = jnp.sum(lv[0:1] * lv[1:2], axis=-1, keepdims=True)
    b = jnp.sum(lv[2:3] * lv[3:4], axis=-1, keepdims=True)
    return jnp.exp(a) - jnp.exp(b) + lam_init


def _ones_row_block(tk):
    return (lax.broadcasted_iota(I32, (BF16_ROWS, tk), 0) == 0).astype(BF16)


def _visible_strip(q0, k0, tk):
    c = k0 + lax.broadcasted_iota(I32, (tk, LANE), 0)
    r = q0 + lax.broadcasted_iota(I32, (tk, LANE), 1)
    return (c >> CHUNK_SHIFT) <= (r >> CHUNK_SHIFT)


def _weights_and_rescale(s_ref, p_ref, m_prev, mx):
    m_new = jnp.maximum(m_prev, mx)
    p_ref[...] = jnp.exp2(s_ref[...] - m_new).astype(BF16)
    return m_new, jnp.exp2(m_prev - m_new)


def _sweep_key_tiles(scores, values, s_sc, p_sc, acc_sc, q0, tq, tk, unroll):
    width = s_sc.shape[2]
    acc_sc[...] = jnp.zeros(acc_sc.shape, F32)
    p_sc[1] = jnp.zeros(p_sc.shape[1:], BF16)
    n_full = q0 // tk
    s0 = scores(0, False)
    s_sc[0] = s0

    def step(j, cur, carry):
        m_prev, alpha_prev, mx = carry
        s_next = scores(j + 1, False)
        s_sc[1 - cur] = s_next
        acc_sc[...] = alpha_prev * acc_sc[...] + _dot(values(jnp.maximum(j - 1, 0)), p_sc[1 - cur])
        m_new, alpha = _weights_and_rescale(s_sc.at[cur], p_sc.at[cur], m_prev, mx)
        return m_new, alpha, jnp.max(s_next, axis=0, keepdims=True)

    def steps(n, base, carry):
        for u in range(n):
            carry = step(base + u, u % 2, carry)
        return carry

    carry = (jnp.full((1, width), NEG, F32), jnp.ones((1, width), F32), jnp.max(s0, axis=0, keepdims=True))
    carry = lax.fori_loop(0, n_full // unroll, lambda jj, c: steps(unroll, unroll * jj, c), carry)
    n = unroll // 2
    while n:
        base = n_full // (2 * n) * (2 * n)
        carry = lax.fori_loop(0, (n_full // n) % 2, lambda _, c, n=n, base=base: steps(n, base, c), carry)
        n //= 2
    m, alpha, _ = carry
    last = (n_full + 1) % 2
    masked = [scores(n_full + j, True) for j in range(max(1, tq // tk))]
    acc_sc[...] = alpha * acc_sc[...] + _dot(values(jnp.maximum(n_full - 1, 0)), p_sc[last])
    for j, s in enumerate(masked):
        m_new = jnp.maximum(m, jnp.max(s, axis=0, keepdims=True))
        p = jnp.exp2(s - m_new).astype(BF16)
        acc_sc[...] = jnp.exp2(m - m_new) * acc_sc[...] + _dot(values(n_full + j), p)
        m = m_new
    return acc_sc[...]


def _mla_prompt_body(q_ref, k_ref, vt_ref, o_ref, s_sc, p_sc, acc_sc, *, tq, tk, unroll):
    q0 = pl.program_id(1) * tq
    q = q_ref[...]
    ones = _ones_row_block(tk)

    def key_start(kt):
        return pl.multiple_of(kt * tk, tk)

    def scores(kt, masked):
        s = _dot_nt(k_ref[pl.ds(key_start(kt), tk), :], q)
        if masked:
            vis = jnp.concatenate([_visible_strip(q0 + c * LANE, kt * tk, tk) for c in range(tq // LANE)], axis=1)
            s = jnp.where(vis, s, NEG)
        return s

    def values(kt):
        return jnp.concatenate([vt_ref[:, pl.ds(key_start(kt), tk)], ones], axis=0)

    acc = _sweep_key_tiles(scores, values, s_sc, p_sc, acc_sc, q0, tq, tk, unroll)
    o = acc[:V_DIM_A] / acc[V_DIM_A:V_DIM_A + 1]
    o_ref[:, :V_DIM_A] = o.T.astype(o_ref.dtype)
    o_ref[:, V_DIM_A:] = jnp.zeros((tq, LANE - V_DIM_A), o_ref.dtype)


def _diff_prompt_body(q_ref, k_ref, vt_ref, slope_ref, lamv_ref, g_ref, o_ref, s_sc, p_sc, acc_sc,
                      *, tq, tk, unroll, lam_init):
    q0 = pl.program_id(1) * tq
    qs = _stack_diff_queries(q_ref[...])
    slope = slope_ref[0][:, :1] * LOG2E
    ones = _ones_row_block(tk)

    def key_start(kt):
        return pl.multiple_of(kt * tk, tk)

    def scores(kt, masked):
        s = _dot_nt(k_ref[pl.ds(key_start(kt), tk), :], qs)
        c_rel = (kt * tk - q0 + lax.broadcasted_iota(I32, (tk, LANE), 0)).astype(F32)
        if not masked:
            return s + jnp.concatenate([slope * c_rel] * (2 * tq // LANE), axis=1)
        strips = []
        for c in range(tq // LANE):
            a = (c * LANE + lax.broadcasted_iota(I32, (tk, LANE), 1)).astype(F32)
            bias = slope * jnp.minimum(c_rel, 2.0 * a - c_rel)
            strips.append(jnp.where(_visible_strip(q0 + c * LANE, kt * tk, tk), bias, NEG))
        return s + jnp.concatenate(strips + strips, axis=1)

    def values(kt):
        return jnp.concatenate([vt_ref[:, pl.ds(key_start(kt), tk)], ones], axis=0)

    acc = _sweep_key_tiles(scores, values, s_sc, p_sc, acc_sc, q0, tq, tk, unroll)
    dv = 2 * DH_B
    o1 = acc[:dv, :tq] / acc[dv:dv + 1, :tq]
    o2 = acc[:dv, tq:] / acc[dv:dv + 1, tq:]
    o = o1 - _diff_lambda(lamv_ref, lam_init) * o2
    o = o * lax.rsqrt(jnp.mean(o * o, axis=0, keepdims=True) + RMS_EPS)
    o_ref[...] = (o.T * g_ref[...] * (1.0 - lam_init)).astype(o_ref.dtype)


def _prompt_attn_call(body, q, k, vt, extra, extra_specs, *, maps, tq, tk, unroll, name):
    t = q.shape[0]
    dv = vt.shape[0] // H_A
    width = maps * tq
    assert t % tq == 0 and t % tk == 0 and (tq % tk == 0 or tk % tq == 0) and tq % LANE == 0 and tk % CHUNK == 0
    qspec = pl.BlockSpec((tq, LANE), lambda h, i: (i, h))
    return pl.pallas_call(
        functools.partial(body, tq=tq, tk=tk, unroll=unroll),
        out_shape=jax.ShapeDtypeStruct((t, HEAD_COLS), BF16),
        grid=(H_A, t // tq),
        in_specs=[qspec, pl.BlockSpec((t, LANE), lambda h, i: (0, h)),
                  pl.BlockSpec((dv, t), lambda h, i: (h, 0))] + extra_specs,
        out_specs=qspec,
        scratch_shapes=[pltpu.VMEM((2, tk, width), F32), pltpu.VMEM((2, tk, width), BF16),
                        pltpu.VMEM((dv + BF16_ROWS, width), F32)],
        compiler_params=_params(("parallel", "arbitrary")),
        name=name,
    )(q, k, vt, *extra)


def _chunk_visible(q0, k0, shape):
    r = q0 + lax.broadcasted_iota(I32, shape, 0)
    c = k0 + lax.broadcasted_iota(I32, shape, 1)
    return (c >> CHUNK_SHIFT) <= (r >> CHUNK_SHIFT)


def _two_part_softmax_pv(s_past, s_new, v_past, v_new):
    m = jnp.maximum(jnp.max(s_past, axis=-1, keepdims=True), jnp.max(s_new, axis=-1, keepdims=True))
    p_past = jnp.exp2(s_past - m)
    p_new = jnp.exp2(s_new - m)
    l = jnp.sum(p_past, axis=-1, keepdims=True) + jnp.sum(p_new, axis=-1, keepdims=True)
    acc = _dot(p_past.astype(BF16), v_past) + _dot(p_new.astype(BF16), v_new)
    return acc / l


def _mla_sample_body(q_ref, kp_ref, vp_ref, kn_ref, vn_ref, o_ref, *, past):
    q = q_ref[...]
    s_past = _dot_nt(q, kp_ref[...])
    s_new = _dot_nt(q, kn_ref[...])
    s_past = jnp.where(_chunk_visible(past, 0, s_past.shape), s_past, NEG)
    s_new = jnp.where(_chunk_visible(past, past, s_new.shape), s_new, NEG)
    o_ref[...] = _two_part_softmax_pv(s_past, s_new, vp_ref[...], vn_ref[...]).astype(o_ref.dtype)


def _diff_sample_body(q_ref, kp_ref, vp_ref, kn_ref, vn_ref, slope_ref, lamv_ref, g_ref, o_ref,
                      *, past, lam_init):
    sq = q_ref.shape[0]
    qs = _stack_diff_queries(q_ref[...])
    slope = slope_ref[0][:, :1] * LOG2E

    def scores(k, k0):
        n = k.shape[0]
        s = _dot_nt(qs, k.astype(BF16))
        qpos = past + lax.broadcasted_iota(I32, (sq, n), 0)
        kpos = k0 + lax.broadcasted_iota(I32, (sq, n), 1)
        bias = -slope * jnp.abs(qpos - kpos).astype(F32)
        vis = _chunk_visible(past, k0, (sq, n))
        bias = jnp.concatenate([bias, bias], axis=0)
        vis = jnp.concatenate([vis, vis], axis=0)
        return jnp.where(vis, s + bias, NEG)

    o = _two_part_softmax_pv(scores(kp_ref[...], 0), scores(kn_ref[...], past),
                             vp_ref[...].astype(BF16), vn_ref[...])
    lam = _diff_lambda(lamv_ref, lam_init)
    o = o[:sq] - lam * o[sq:]
    o_ref[...] = (_rms_norm(o, g_ref[...], RMS_EPS) * (1.0 - lam_init)).astype(o_ref.dtype)


def _sample_attn_call(body, q, kp, vp, kn, vn, extra, extra_specs, *, batch, past, name):
    sq = q.shape[0] // batch
    assert past % CHUNK == 0 and kp.shape[0] == batch * past
    new = pl.BlockSpec((sq, LANE), lambda b, h: (b, h))
    old = pl.BlockSpec((past, LANE), lambda b, h: (b, h))
    return pl.pallas_call(
        functools.partial(body, past=past),
        out_shape=jax.ShapeDtypeStruct((batch * sq, HEAD_COLS), BF16),
        grid=(batch, H_A),
        in_specs=[new, old, old, new, new] + extra_specs,
        out_specs=new,
        compiler_params=_params(("parallel", "parallel")),
        name=name,
    )(q, kp, vp, kn, vn, *extra)


def _top4_of_row(logits):
    lane = lax.broadcasted_iota(I32, logits.shape, 1)
    vals = logits
    top_v, top_i = [], []
    for _ in range(TOP_K):
        mx = jnp.max(vals, axis=-1, keepdims=True)
        sel = jnp.min(jnp.where(vals == mx, lane, N_EXPERTS), axis=-1, keepdims=True)
        top_v.append(mx)
        top_i.append(sel)
        vals = jnp.where(lane == sel, -jnp.inf, vals)
    e = [jnp.exp(v - top_v[0]) for v in top_v]
    tot = e[0] + e[1] + e[2] + e[3]
    return top_i, [x / tot for x in e]


def _outproj_body(oa_ref, ob_ref, sga_ref, sgb_ref, x_ref, woa_ref, wob_ref, wout_ref, g_ref, b_ref,
                  wr_ref, br_ref, x1_ref, x1t_ref, idx_ref, gate_ref):
    a = _dot(oa_ref[...], woa_ref[...])
    b = _dot(ob_ref[...], wob_ref[...])
    merged = sga_ref[...].astype(F32) * a + sgb_ref[...].astype(F32) * b
    y = _dot(merged.astype(BF16), wout_ref[...])
    x1 = _layer_norm(ALPHA * x_ref[...] + y, g_ref[...], b_ref[...])
    x1_ref[...] = x1
    _store_row_tiles(x1t_ref, x1)
    logits = jnp.dot(x1, wr_ref[...], preferred_element_type=F32, precision=lax.Precision.HIGHEST) + br_ref[...]
    top_i, gates = _top4_of_row(logits)
    lane = lax.broadcasted_iota(I32, idx_ref.shape, 1)
    idx = jnp.zeros(idx_ref.shape, I32)
    gate = jnp.zeros(gate_ref.shape, F32)
    for k in range(TOP_K):
        idx = jnp.where(lane == k, top_i[k], idx)
        gate = jnp.where(lane == k, gates[k], gate)
    idx_ref[...] = idx
    gate_ref[...] = gate


def _outproj_call(oa, ob, sga, sgb, x, woa, wob, wout, g, b, wr, br):
    t, d = x.shape
    tm = _pick_tile(t, 256)
    row = lambda n: pl.BlockSpec((tm, n), lambda i: (i, 0))
    const = lambda a: pl.BlockSpec(a.shape, lambda i: (0, 0))
    return pl.pallas_call(
        _outproj_body,
        out_shape=(jax.ShapeDtypeStruct((t, d), F32), jax.ShapeDtypeStruct((t, d // LANE, LANE), F32),
                   jax.ShapeDtypeStruct((t, LANE), I32), jax.ShapeDtypeStruct((t, LANE), F32)),
        grid=(t // tm,),
        in_specs=[row(HEAD_COLS), row(HEAD_COLS), row(HEAD_COLS), row(HEAD_COLS), row(d),
                  const(woa), const(wob), const(wout), const(g), const(b), const(wr), const(br)],
        out_specs=(row(d), pl.BlockSpec((tm, d // LANE, LANE), lambda i: (i, 0, 0)), row(LANE), row(LANE)),
        compiler_params=_params(("parallel",)),
        name="outproj",
    )(oa, ob, sga, sgb, x, woa, wob, wout, g, b, wr, br)


def _store_row_tiles(ref, x):
    for j in range(x.shape[1] // LANE):
        ref[:, j, :] = x[:, j * LANE:(j + 1) * LANE]


def _load_row_tiles(ref):
    return jnp.concatenate([ref[:, j, :] for j in range(ref.shape[1])], axis=1)


def _rank_body(idx_ref, rank_ref, cnt_ref, run_sc):
    tr = idx_ref.shape[0]

    @pl.when(pl.program_id(0) == 0)
    def _():
        run_sc[...] = jnp.zeros(run_sc.shape, F32)

    idx = idx_ref[...]
    lane = lax.broadcasted_iota(I32, (tr, LANE), 1)
    chosen = [idx[:, k:k + 1] == lane for k in range(TOP_K)]
    member = functools.reduce(lambda a, b: a | b, chosen).astype(F32)
    below = (lax.broadcasted_iota(I32, (tr, tr), 0) > lax.broadcasted_iota(I32, (tr, tr), 1)).astype(BF16)
    before = _dot(below, member.astype(BF16)) + run_sc[...]
    rank = jnp.zeros((tr, LANE), F32)
    for k in range(TOP_K):
        rank = jnp.where(lane == k, jnp.sum(jnp.where(chosen[k], before, 0.0), axis=-1, keepdims=True), rank)
    rank_ref[...] = rank.astype(I32)
    run_sc[...] = run_sc[...] + jnp.sum(member, axis=0, keepdims=True)
    cnt_ref[...] = run_sc[...]


def _rank_call(top_idx):
    t = top_idx.shape[0]
    tr = _pick_tile(t, 256)
    row = pl.BlockSpec((tr, LANE), lambda i: (i, 0))
    return pl.pallas_call(
        _rank_body,
        out_shape=(jax.ShapeDtypeStruct((t, LANE), I32), jax.ShapeDtypeStruct((1, LANE), F32)),
        grid=(t // tr,),
        in_specs=[row],
        out_specs=(row, pl.BlockSpec((1, LANE), lambda i: (0, 0))),
        scratch_shapes=[pltpu.VMEM((1, LANE), F32)],
        compiler_params=_params(("arbitrary",)),
        name="moe_rank",
    )(top_idx)


def _route(top_idx, bm):
    t = top_idx.shape[0]
    rank, counts = _rank_call(top_idx)
    counts = counts[0, :N_EXPERTS].astype(I32)
    padded = (counts + bm - 1) // bm * bm
    pad_end = jnp.cumsum(padded)
    pad_start = pad_end - padded
    n_blocks = -(-t * TOP_K // bm) + N_EXPERTS
    block_e = jnp.minimum(
        jnp.sum(pad_end[None, :] <= (jnp.arange(n_blocks, dtype=I32) * bm)[:, None], axis=1), N_EXPERTS - 1).astype(I32)
    n_used = (pad_end[-1:] // bm).astype(I32)
    idx = top_idx[:, :TOP_K]
    start_of = jnp.sum(jnp.where(idx[:, :, None] == jnp.arange(N_EXPERTS, dtype=I32), pad_start, 0), axis=-1)
    return block_e, n_used, (start_of + rank[:, :TOP_K]).astype(I32)


def _dispatch_body(dest_ref, x_hbm, xs_in_hbm, xs_hbm, sem, *, td):
    del xs_in_hbm
    i = pl.program_id(0)
    n = pl.num_programs(0)
    rows = TOP_K * td

    def copy(r, tok, slot):
        return pltpu.make_async_copy(x_hbm.at[tok], xs_hbm.at[dest_ref[0, 0, r]], sem.at[slot])

    def start_all(slot):
        def body(r, c):
            copy(r, i * td + r // TOP_K, slot).start()
            return c

        lax.fori_loop(0, rows, body, 0)

    def wait_all(slot):
        def body(r, c):
            pltpu.make_async_copy(x_hbm.at[0], xs_hbm.at[0], sem.at[slot]).wait()
            return c

        lax.fori_loop(0, rows, body, 0)

    start_all(i % 2)

    @pl.when(i > 0)
    def _():
        wait_all(1 - i % 2)

    @pl.when(i == n - 1)
    def _():
        wait_all(i % 2)


def _dispatch_call(dest, x_tiles, xs_buf, *, td):
    t = x_tiles.shape[0]
    n = t // td
    rows = TOP_K * td
    return pl.pallas_call(
        functools.partial(_dispatch_body, td=td),
        out_shape=jax.ShapeDtypeStruct(xs_buf.shape, F32),
        grid=(n,),
        in_specs=[pl.BlockSpec((1, 1, rows), lambda i: (i, 0, 0), memory_space=pltpu.SMEM),
                  pl.BlockSpec(memory_space=pl.ANY), pl.BlockSpec(memory_space=pl.ANY)],
        out_specs=pl.BlockSpec(memory_space=pl.ANY),
        scratch_shapes=[pltpu.SemaphoreType.DMA((2,))],
        input_output_aliases={2: 0},
        compiler_params=_params(("arbitrary",)),
        name="moe_dispatch",
    )(dest.reshape(n, 1, rows), x_tiles, xs_buf)


def _moe_body(be_ref, nb_ref, xs_ref, wgu_ref, bgu_ref, wdn_ref, bdn_ref, y_ref):
    @pl.when(pl.program_id(0) < nb_ref[0])
    def _():
        x = _load_row_tiles(xs_ref).astype(BF16)
        h = _dot(x, wgu_ref[0]) + bgu_ref[0]
        g = jnp.minimum(h[:, :D_FF], SWIGLU_LIMIT)
        u = jnp.clip(h[:, D_FF:], -SWIGLU_LIMIT, SWIGLU_LIMIT)
        act = (u + 1.0) * (g * jax.nn.sigmoid(SWIGLU_ALPHA * g))
        _store_row_tiles(y_ref, _dot(act.astype(BF16), wdn_ref[0]) + bdn_ref[0])

    @pl.when(pl.program_id(0) >= nb_ref[0])
    def _():
        y_ref[...] = jnp.zeros(y_ref.shape, F32)


def _moe_call(block_e, n_used, xs, wgu, bgu, wdn, bdn, *, bm):
    n_blocks = block_e.shape[0]
    d = D_MODEL
    tiles = pl.BlockSpec((bm, d // LANE, LANE), lambda b, be, nb: (b, 0, 0))
    grid_spec = pltpu.PrefetchScalarGridSpec(
        num_scalar_prefetch=2,
        grid=(n_blocks,),
        in_specs=[
            tiles,
            pl.BlockSpec((1, d, 2 * D_FF), lambda b, be, nb: (be[b], 0, 0)),
            pl.BlockSpec((1, 1, 2 * D_FF), lambda b, be, nb: (be[b], 0, 0)),
            pl.BlockSpec((1, D_FF, d), lambda b, be, nb: (be[b], 0, 0)),
            pl.BlockSpec((1, 1, d), lambda b, be, nb: (be[b], 0, 0)),
        ],
        out_specs=tiles,
    )
    return pl.pallas_call(
        _moe_body,
        out_shape=jax.ShapeDtypeStruct(xs.shape, F32),
        grid_spec=grid_spec,
        compiler_params=_params(("arbitrary",)),
        name="moe_experts",
    )(block_e, n_used, xs, wgu, bgu.reshape(N_EXPERTS, 1, 2 * D_FF), wdn, bdn.reshape(N_EXPERTS, 1, d))


def _start_row_gather(idx_ref, n, src_hbm, dst, sem):
    def body(r, c):
        pltpu.make_async_copy(src_hbm.at[idx_ref[0, 0, r]], dst.at[r], sem).start()
        return c

    lax.fori_loop(0, n, body, 0)


def _wait_row_gather(n, src_hbm, dst, sem):
    def body(r, c):
        pltpu.make_async_copy(src_hbm.at[0], dst.at[r], sem).wait()
        return c

    lax.fori_loop(0, n, body, 0)


def _combine_body(dest_ref, destn_ref, y_hbm, x_ref, gate_ref, g_ref, b_ref, o_ref, ybuf, sem, *, tc):
    i = pl.program_id(0)
    n = pl.num_programs(0)
    slot = i % 2
    rows = TOP_K * tc

    @pl.when(i == 0)
    def _():
        _start_row_gather(dest_ref, rows, y_hbm, ybuf.at[0], sem.at[0])

    @pl.when(i + 1 < n)
    def _():
        _start_row_gather(destn_ref, rows, y_hbm, ybuf.at[1 - slot], sem.at[1 - slot])

    _wait_row_gather(rows, y_hbm, ybuf.at[slot], sem.at[slot])
    gate = gate_ref[...]
    moe = jnp.zeros(x_ref.shape, F32)
    for k in range(TOP_K):
        moe = moe + gate[:, k:k + 1] * _load_row_tiles(ybuf.at[slot, k * tc:(k + 1) * tc])
    o_ref[...] = _layer_norm(ALPHA * x_ref[...] + moe, g_ref[...], b_ref[...])


def _combine_call(dest, y, x, gate, g, b, *, tc):
    t, d = x.shape
    n = t // tc
    rows = TOP_K * tc
    dest3 = dest.reshape(n, tc, TOP_K).transpose(0, 2, 1).reshape(n, 1, rows)
    smem_row = lambda f: pl.BlockSpec((1, 1, rows), f, memory_space=pltpu.SMEM)
    row = lambda w: pl.BlockSpec((tc, w), lambda i: (i, 0))
    const = lambda a: pl.BlockSpec(a.shape, lambda i: (0, 0))
    return pl.pallas_call(
        functools.partial(_combine_body, tc=tc),
        out_shape=jax.ShapeDtypeStruct((t, d), F32),
        grid=(n,),
        in_specs=[smem_row(lambda i: (i, 0, 0)), smem_row(lambda i: (jnp.minimum(i + 1, n - 1), 0, 0)),
                  pl.BlockSpec(memory_space=pl.ANY), row(d), row(LANE), const(g), const(b)],
        out_specs=row(d),
        scratch_shapes=[pltpu.VMEM((2, rows, d // LANE, LANE), F32), pltpu.SemaphoreType.DMA((2,))],
        compiler_params=_params(("arbitrary",)),
        name="moe_combine",
    )(dest3, dest3, y, x, gate, g, b)


def _head_blocks(w, widths):
    per = sum(widths)
    w = w.reshape(*w.shape[:-1], H_A, per)
    pad = [(0, 0)] * (w.ndim - 1) + [(0, LANE - per)]
    return jnp.pad(w, pad).reshape(*w.shape[:-2], HEAD_COLS)


def _rotate_half_cols(w):
    half = ROPE_DIM // 2
    return jnp.concatenate([-w[..., half:], w[..., :half]], axis=-1)


def _prep_layer(w_in, w_uq, w_ukv, w_oa):
    d = w_in.shape[0]
    splits = (Q_LORA, KV_LORA, ROPE_DIM, 1024, 1024, 1024, D_MODEL, D_MODEL)
    offs = [0]
    for s in splits:
        offs.append(offs[-1] + s)
    lat = w_in[:, offs[0]:offs[2]]
    kr = w_in[:, offs[2]:offs[3]]
    rest = w_in[:, offs[3]:]

    def rope_block(w):
        return jnp.pad(w, ((0, 0), (NOPE_DIM, LANE - NOPE_DIM - ROPE_DIM)))

    w1 = jnp.concatenate([lat, rope_block(kr), rope_block(_rotate_half_cols(kr)), rest], axis=1).astype(BF16)
    assert w1.shape == (d, C_END)
    wvt = w_in[:, offs[5]:offs[6]].T.astype(BF16)

    uq = w_uq.reshape(Q_LORA, H_A, NOPE_DIM + ROPE_DIM)
    uq_rot = jnp.concatenate([jnp.zeros((Q_LORA, H_A, NOPE_DIM), F32), _rotate_half_cols(uq[..., NOPE_DIM:])], -1)
    wuq = jnp.concatenate([_head_blocks(uq.reshape(Q_LORA, -1), (NOPE_DIM + ROPE_DIM,)),
                           _head_blocks(uq_rot.reshape(Q_LORA, -1), (NOPE_DIM + ROPE_DIM,))], axis=1).astype(BF16)

    ukv = w_ukv.reshape(KV_LORA, H_A, NOPE_DIM + V_DIM_A)
    wuk = _head_blocks(ukv[..., :NOPE_DIM].reshape(KV_LORA, -1), (NOPE_DIM,)).astype(BF16)
    wuv = _head_blocks(ukv[..., NOPE_DIM:].reshape(KV_LORA, -1), (V_DIM_A,)).astype(BF16)
    wuvt = ukv[..., NOPE_DIM:].reshape(KV_LORA, H_A * V_DIM_A).T.astype(BF16)

    woa = jnp.pad(w_oa.reshape(H_A, V_DIM_A, D_MODEL), ((0, 0), (0, LANE - V_DIM_A), (0, 0)))
    woa = woa.reshape(HEAD_COLS, D_MODEL).astype(BF16)
    return w1, wvt, wuq, wuk, wuv, wuvt, woa


def _rope_tables(pos):
    half = ROPE_DIM // 2
    inv = ROPE_THETA ** (-jnp.arange(half, dtype=F32) / half)
    ang = pos.astype(F32)[:, None] * inv[None, :]
    cos, sin = jnp.cos(ang), jnp.sin(ang)
    n = pos.shape[0]
    tail = jnp.zeros((n, LANE - NOPE_DIM - ROPE_DIM), F32)
    cos_t = jnp.concatenate([jnp.ones((n, NOPE_DIM), F32), cos, cos, tail], axis=1)
    sin_t = jnp.concatenate([jnp.zeros((n, NOPE_DIM), F32), sin, sin, tail], axis=1)
    return cos_t, sin_t


def _rope_block_of(kr):
    return jnp.pad(kr, ((0, 0), (NOPE_DIM, LANE - NOPE_DIM - ROPE_DIM)))


def _moe_block_rows(t):
    return 256 if t * TOP_K >= 256 * N_EXPERTS else 128


def _moe_row_buffer(t):
    bm = _moe_block_rows(t)
    return jnp.zeros(((-(-t * TOP_K // bm) + N_EXPERTS) * bm, D_MODEL // LANE, LANE), F32)


def _moe_layer(x1, x1_tiles, top_idx, gate, rows_buf, wgu, bgu, wdn, bdn, g, b):
    t = x1.shape[0]
    bm = _moe_block_rows(t)
    tc = _pick_tile(t, 128)
    block_e, n_used, dest = _route(top_idx, bm)
    xs = _dispatch_call(dest, x1_tiles, rows_buf, td=tc)
    y = _moe_call(block_e, n_used, xs, wgu, bgu, wdn, bdn, bm=bm)
    return _combine_call(dest, y, x1, gate, g, b, tc=tc), y


def kernel(x_prompt, x_sample, cache_mla_latent, cache_mla_krope, cache_diff_k, cache_diff_v,
           ln_in_g, ln_in_b, w_in, q_norm_g, kv_norm_g, w_uq, w_ukv, lam_q1, lam_k1, lam_q2, lam_k2,
           subln_g, w_oa, w_ob, w_out, ln1_g, ln1_b, w_router, b_router, w_gu, b_gu, w_dn, b_dn,
           ln2_g, ln2_b):
    bp, sp, d = x_prompt.shape
    bs, ss, _ = x_sample.shape
    past = cache_mla_latent.shape[2]
    assert bp == 1 and d == D_MODEL
    tp, ts = bp * sp, bs * ss

    cos_p, sin_p = _rope_tables(jnp.arange(sp, dtype=I32))
    cos_s, sin_s = _rope_tables(jnp.tile(past + jnp.arange(ss, dtype=I32), bs))
    slopes = 2.0 ** (-8.0 * (jnp.arange(H_B, dtype=F32) + 1.0) / H_B)
    slopes = jnp.broadcast_to(slopes[:, None, None], (H_B, 1, LANE))
    tk = _pick_tile(sp, 256)
    tq_mla = _pick_tile(sp, 512)
    tq_diff = _pick_tile(sp, 256)

    xp = _ln_call(x_prompt.reshape(tp, d), ln_in_g, ln_in_b)
    xs = _ln_call(x_sample.reshape(ts, d), ln_in_g, ln_in_b)
    rows_p, rows_s = _moe_row_buffer(tp), _moe_row_buffer(ts)
    new_p = ([], [], [], [])
    new_s = ([], [], [], [])
    for l in range(DEPTH):
        lam_init = 0.8 - 0.6 * math.exp(-0.3 * l)
        w1, wvt, wuq, wuk, wuv, wuvt, woa = _prep_layer(w_in[l], w_uq[l], w_ukv[l], w_oa[l])
        gq = q_norm_g[l].reshape(1, Q_LORA)
        gkv = kv_norm_g[l].reshape(1, KV_LORA)
        lamv = jnp.stack([lam_q1[l], lam_k1[l], lam_q2[l], lam_k2[l]])
        subg = subln_g[l].reshape(1, 2 * DH_B)
        wob = w_ob[l].astype(BF16)
        wout = w_out[l].astype(BF16)
        g1, b1 = ln1_g[l].reshape(1, d), ln1_b[l].reshape(1, d)
        g2, b2 = ln2_g[l].reshape(1, d), ln2_b[l].reshape(1, d)
        wr, br = w_router[l], b_router[l].reshape(1, N_EXPERTS)
        wgu, wdn = w_gu[l].astype(BF16), w_dn[l].astype(BF16)
        diff_extra = [slopes, lamv, subg]

        def diff_specs(head_of):
            return [pl.BlockSpec((1, 1, LANE), lambda a, b: (head_of(a, b), 0, 0)),
                    pl.BlockSpec(lamv.shape, lambda a, b: (0, 0)), pl.BlockSpec(subg.shape, lambda a, b: (0, 0))]

        qa, lat, krb, qd, kd32, kd16, vd32, _, vdt, sga, sgb = _inproj_call(xp, w1, wvt, wuq, gq, gkv, cos_p, sin_p)
        ka, vat = _kvup_call(lat, krb, wuk, wuvt, feature_major_v=True)
        oa = _prompt_attn_call(_mla_prompt_body, qa, ka, vat, [], [], maps=1, tq=tq_mla, tk=tk, unroll=MLA_UNROLL,
                               name="mla_prompt")
        ob = _prompt_attn_call(functools.partial(_diff_prompt_body, lam_init=lam_init), qd, kd16, vdt,
                               diff_extra, diff_specs(lambda h, i: h), maps=2, tq=tq_diff, tk=tk, unroll=DIFF_UNROLL,
                               name="diff_prompt")
        xp1, xp1t, idx_p, gate_p = _outproj_call(oa, ob, sga, sgb, xp, woa, wob, wout, g1, b1, wr, br)
        new_p[0].append(lat.reshape(bp, sp, KV_LORA))
        new_p[1].append(krb[:, NOPE_DIM:NOPE_DIM + ROPE_DIM].reshape(bp, sp, ROPE_DIM))
        new_p[2].append(kd32.reshape(bp, sp, 2 * H_B, DH_B))
        new_p[3].append(vd32.reshape(bp, sp, H_B, 2 * DH_B))

        qa, lat, krb, qd, kd32, kd16, vd32, vd16, _, sga, sgb = _inproj_call(xs, w1, wvt, wuq, gq, gkv, cos_s, sin_s)
        kn, vn = _kvup_call(lat, krb, wuk, wuv, feature_major_v=False)
        kpast, vpast = _kvup_call(cache_mla_latent[l].reshape(bs * past, KV_LORA),
                                  _rope_block_of(cache_mla_krope[l].reshape(bs * past, ROPE_DIM)), wuk, wuv,
                                  feature_major_v=False)
        oa = _sample_attn_call(_mla_sample_body, qa, kpast, vpast, kn, vn, [], [], batch=bs, past=past,
                               name="mla_sample")
        ob = _sample_attn_call(functools.partial(_diff_sample_body, lam_init=lam_init), qd,
                               cache_diff_k[l].reshape(bs * past, HEAD_COLS),
                               cache_diff_v[l].reshape(bs * past, HEAD_COLS), kd16, vd16,
                               diff_extra, diff_specs(lambda b, h: h), batch=bs, past=past, name="diff_sample")
        xs1, xs1t, idx_s, gate_s = _outproj_call(oa, ob, sga, sgb, xs, woa, wob, wout, g1, b1, wr, br)
        new_s[0].append(lat.reshape(bs, ss, KV_LORA))
        new_s[1].append(krb[:, NOPE_DIM:NOPE_DIM + ROPE_DIM].reshape(bs, ss, ROPE_DIM))
        new_s[2].append(kd32.reshape(bs, ss, 2 * H_B, DH_B))
        new_s[3].append(vd32.reshape(bs, ss, H_B, 2 * DH_B))

        xp, rows_p = _moe_layer(xp1, xp1t, idx_p, gate_p, rows_p, wgu, b_gu[l], wdn, b_dn[l], g2, b2)
        xs, rows_s = _moe_layer(xs1, xs1t, idx_s, gate_s, rows_s, wgu, b_gu[l], wdn, b_dn[l], g2, b2)

    return (xp.reshape(bp, sp, d), xs.reshape(bs, ss, d),
            jnp.stack(new_p[0]), jnp.stack(new_p[1]), jnp.stack(new_p[2]), jnp.stack(new_p[3]),
            jnp.stack(new_s[0]), jnp.stack(new_s[1]), jnp.stack(new_s[2]), jnp.stack(new_s[3]))
```

```python
import functools
import math

import jax
import jax.numpy as jnp
from jax import lax
from jax.experimental import pallas as pl
from jax.experimental.pallas import tpu as pltpu

F32 = jnp.float32
BF16 = jnp.bfloat16
I32 = jnp.int32

D_MODEL = 1024
DEPTH = 4
CHUNK = 64
CHUNK_SHIFT = 6
H_A = 8
NOPE_DIM = 64
ROPE_DIM = 32
V_DIM_A = 64
Q_LORA = 256
KV_LORA = 256
ROPE_THETA = 10000.0
H_B = 8
DH_B = 64
N_EXPERTS = 32
TOP_K = 4
D_FF = 1024
SWIGLU_LIMIT = 7.0
SWIGLU_ALPHA = 1.702
LN_EPS = 1e-5
RMS_EPS = 1e-6
ALPHA = (2 * DEPTH) ** 0.25

LANE = 128
BF16_ROWS = 16
HEAD_COLS = H_A * LANE
LOG2E = 1.4426950408889634
MLA_QSCALE = (NOPE_DIM + ROPE_DIM) ** -0.5 * LOG2E
DIFF_QSCALE = DH_B ** -0.5 * LOG2E
MLA_UNROLL = 8
DIFF_UNROLL = 4
NEG = -1e30
VMEM_LIMIT = 48 * 1024 * 1024

C_LAT = 0
C_KR = 512
C_KRROT = 640
C_QD = 768
C_KD = C_QD + 1024
C_VD = C_KD + 1024
C_GA = C_VD + 1024
C_GB = C_GA + 1024
C_END = C_GB + 1024


def _pick_tile(n, pref):
    t = min(n, pref)
    while n % t:
        t -= 8
    assert t > 0 and (t % 8 == 0 or t == n), (n, pref)
    return t


def _params(sem):
    return pltpu.CompilerParams(dimension_semantics=sem, vmem_limit_bytes=VMEM_LIMIT)


def _layer_norm(x, g, b):
    mu = jnp.mean(x, axis=-1, keepdims=True)
    xc = x - mu
    var = jnp.mean(xc * xc, axis=-1, keepdims=True)
    return xc * lax.rsqrt(var + LN_EPS) * g + b


def _rms_norm(x, g, eps):
    ms = jnp.mean(x * x, axis=-1, keepdims=True)
    return x * lax.rsqrt(ms + eps) * g


def _dot(a, b):
    return jnp.dot(a, b, preferred_element_type=F32)


def _dot_nt(a, b):
    return lax.dot_general(a, b, (((1,), (1,)), ((), ())), preferred_element_type=F32)


def _ln_body(x_ref, g_ref, b_ref, o_ref):
    o_ref[...] = _layer_norm(x_ref[...], g_ref[...], b_ref[...])


def _ln_call(x, g, b):
    t, d = x.shape
    tm = _pick_tile(t, 512)
    row = pl.BlockSpec((tm, d), lambda i: (i, 0))
    vec = pl.BlockSpec((1, d), lambda i: (0, 0))
    return pl.pallas_call(
        _ln_body,
        out_shape=jax.ShapeDtypeStruct((t, d), F32),
        grid=(t // tm,),
        in_specs=[row, vec, vec],
        out_specs=row,
        compiler_params=_params(("parallel",)),
        name="ln_in",
    )(x, g.reshape(1, d), b.reshape(1, d))


def _inproj_body(x_ref, w_ref, wvt_ref, wuq_ref, gq_ref, gkv_ref, cos_ref, sin_ref,
                 q_ref, lat_ref, krb_ref, qd_ref, kd32_ref, kd16_ref, vd32_ref, vd16_ref, vdt_ref,
                 sga_ref, sgb_ref):
    xb = x_ref[...].astype(BF16)
    cos = cos_ref[...]
    sin = sin_ref[...]

    def proj(lo, hi):
        return _dot(xb, w_ref[:, lo:hi])

    z = proj(C_LAT, C_QD)
    cq = _rms_norm(z[:, :Q_LORA], gq_ref[...], RMS_EPS)
    lat_ref[...] = _rms_norm(z[:, Q_LORA:C_KR], gkv_ref[...], RMS_EPS)
    krb_ref[...] = z[:, C_KR:C_KRROT] * cos + z[:, C_KRROT:C_QD] * sin
    qq = _dot(cq.astype(BF16), wuq_ref[...])
    for h in range(H_A):
        lo = h * LANE
        qh = qq[:, lo:lo + LANE] * cos + qq[:, HEAD_COLS + lo:HEAD_COLS + lo + LANE] * sin
        q_ref[:, lo:lo + LANE] = (qh * MLA_QSCALE).astype(BF16)
    qd_ref[...] = (proj(C_QD, C_KD) * DIFF_QSCALE).astype(BF16)
    z = proj(C_KD, C_VD)
    kd32_ref[...] = z
    kd16_ref[...] = z.astype(BF16)
    z = proj(C_VD, C_GA)
    vd32_ref[...] = z
    vd16_ref[...] = z.astype(BF16)
    vdt_ref[...] = _dot_nt(wvt_ref[...], xb).astype(BF16)
    sga_ref[...] = jax.nn.sigmoid(proj(C_GA, C_GB)).astype(BF16)
    sgb_ref[...] = jax.nn.sigmoid(proj(C_GB, C_END)).astype(BF16)


def _inproj_call(x, w, wvt, wuq, gq, gkv, cos_t, sin_t):
    t, d = x.shape
    tm = _pick_tile(t, 256)
    row = lambda n: pl.BlockSpec((tm, n), lambda i: (i, 0))
    const = lambda a: pl.BlockSpec(a.shape, lambda i: (0, 0), pipeline_mode=pl.Buffered(1))
    wide = lambda dt: jax.ShapeDtypeStruct((t, HEAD_COLS), dt)
    return pl.pallas_call(
        _inproj_body,
        out_shape=(wide(BF16), jax.ShapeDtypeStruct((t, KV_LORA), F32), jax.ShapeDtypeStruct((t, LANE), F32),
                   wide(BF16), wide(F32), wide(BF16), wide(F32), wide(BF16),
                   jax.ShapeDtypeStruct((HEAD_COLS, t), BF16), wide(BF16), wide(BF16)),
        grid=(t // tm,),
        in_specs=[row(d), const(w), const(wvt), const(wuq), const(gq), const(gkv), row(LANE), row(LANE)],
        out_specs=(row(HEAD_COLS), row(KV_LORA), row(LANE), row(HEAD_COLS), row(HEAD_COLS), row(HEAD_COLS),
                   row(HEAD_COLS), row(HEAD_COLS), pl.BlockSpec((HEAD_COLS, tm), lambda i: (0, i)),
                   row(HEAD_COLS), row(HEAD_COLS)),
        compiler_params=_params(("parallel",)),
        name="inproj",
    )(x, w, wvt, wuq, gq, gkv, cos_t, sin_t)


def _kvup_body(lat_ref, krb_ref, wk_ref, wv_ref, k_ref, v_ref, *, feature_major_v):
    lat = lat_ref[...].astype(BF16)
    kn = _dot(lat, wk_ref[...])
    krb = krb_ref[...]
    for h in range(H_A):
        lo = h * LANE
        k_ref[:, lo:lo + LANE] = (kn[:, lo:lo + LANE] + krb).astype(BF16)
    if feature_major_v:
        v_ref[...] = _dot_nt(wv_ref[...], lat).astype(BF16)
    else:
        v_ref[...] = _dot(lat, wv_ref[...]).astype(BF16)


def _kvup_call(lat, krb, wk, wv, *, feature_major_v):
    t = lat.shape[0]
    tm = _pick_tile(t, 512)
    row = lambda n: pl.BlockSpec((tm, n), lambda i: (i, 0))
    const = lambda a: pl.BlockSpec(a.shape, lambda i: (0, 0))
    if feature_major_v:
        v_shape, v_spec = (H_A * V_DIM_A, t), pl.BlockSpec((H_A * V_DIM_A, tm), lambda i: (0, i))
    else:
        v_shape, v_spec = (t, HEAD_COLS), row(HEAD_COLS)
    return pl.pallas_call(
        functools.partial(_kvup_body, feature_major_v=feature_major_v),
        out_shape=(jax.ShapeDtypeStruct((t, HEAD_COLS), BF16), jax.ShapeDtypeStruct(v_shape, BF16)),
        grid=(t // tm,),
        in_specs=[row(KV_LORA), row(LANE), const(wk), const(wv)],
        out_specs=(row(HEAD_COLS), v_spec),
        compiler_params=_params(("parallel",)),
        name="kvup",
    )(lat, krb, wk, wv)


def _stack_diff_queries(q):
    lane = lax.broadcasted_iota(I32, q.shape, 1)
    zero = jnp.zeros_like(q)
    return jnp.concatenate([jnp.where(lane < DH_B, q, zero), jnp.where(lane >= DH_B, q, zero)], axis=0)


def _diff_lambda(lamv_ref, lam_init):
    lv = lamv_ref[...]
    a = jnp.sum(lv[0:1] * lv[1:2], axis=-1, keepdims=True)
    b = jnp.sum(lv[2:3] * lv[3:4], axis=-1, keepdims=True)
    return jnp.exp(a) - jnp.exp(b) + lam_init


def _ones_row_block(tk):
    return (lax.broadcasted_iota(I32, (BF16_ROWS, tk), 0) == 0).astype(BF16)


def _visible_strip(q0, k0, tk):
    c = k0 + lax.broadcasted_iota(I32, (tk, LANE), 0)
    r = q0 + lax.broadcasted_iota(I32, (tk, LANE), 1)
    return (c >> CHUNK_SHIFT) <= (r >> CHUNK_SHIFT)


def _weights_and_rescale(s_ref, p_ref, m_prev, mx):
    m_new = jnp.maximum(m_prev, mx)
    p_ref[...] = jnp.exp2(s_ref[...] - m_new).astype(BF16)
    return m_new, jnp.exp2(m_prev - m_new)


def _sweep_key_tiles(scores, values, s_sc, p_sc, acc_sc, q0, tq, tk, unroll):
    width = s_sc.shape[2]
    acc_sc[...] = jnp.zeros(acc_sc.shape, F32)
    p_sc[1] = jnp.zeros(p_sc.shape[1:], BF16)
    n_full = q0 // tk
    s0 = scores(0, False)
    s_sc[0] = s0

    def step(j, cur, carry):
        m_prev, alpha_prev, mx = carry
        s_next = scores(j + 1, False)
        s_sc[1 - cur] = s_next
        acc_sc[...] = alpha_prev * acc_sc[...] + _dot(values(jnp.maximum(j - 1, 0)), p_sc[1 - cur])
        m_new, alpha = _weights_and_rescale(s_sc.at[cur], p_sc.at[cur], m_prev, mx)
        return m_new, alpha, jnp.max(s_next, axis=0, keepdims=True)

    def steps(n, base, carry):
        for u in range(n):
            carry = step(base + u, u % 2, carry)
        return carry

    carry = (jnp.full((1, width), NEG, F32), jnp.ones((1, width), F32), jnp.max(s0, axis=0, keepdims=True))
    carry = lax.fori_loop(0, n_full // unroll, lambda jj, c: steps(unroll, unroll * jj, c), carry)
    n = unroll // 2
    while n:
        base = n_full // (2 * n) * (2 * n)
        carry = lax.fori_loop(0, (n_full // n) % 2, lambda _, c, n=n, base=base: steps(n, base, c), carry)
        n //= 2
    m, alpha, _ = carry
    last = (n_full + 1) % 2
    masked = [scores(n_full + j, True) for j in range(max(1, tq // tk))]
    acc_sc[...] = alpha * acc_sc[...] + _dot(values(jnp.maximum(n_full - 1, 0)), p_sc[last])
    for j, s in enumerate(masked):
        m_new = jnp.maximum(m, jnp.max(s, axis=0, keepdims=True))
        p = jnp.exp2(s - m_new).astype(BF16)
        acc_sc[...] = jnp.exp2(m - m_new) * acc_sc[...] + _dot(values(n_full + j), p)
        m = m_new
    return acc_sc[...]


def _mla_prompt_body(q_ref, k_ref, vt_ref, o_ref, s_sc, p_sc, acc_sc, *, tq, tk, unroll):
    q0 = pl.program_id(1) * tq
    q = q_ref[...]
    ones = _ones_row_block(tk)

    def key_start(kt):
        return pl.multiple_of(kt * tk, tk)

    def scores(kt, masked):
        s = _dot_nt(k_ref[pl.ds(key_start(kt), tk), :], q)
        if masked:
            vis = jnp.concatenate([_visible_strip(q0 + c * LANE, kt * tk, tk) for c in range(tq // LANE)], axis=1)
            s = jnp.where(vis, s, NEG)
        return s

    def values(kt):
        return jnp.concatenate([vt_ref[:, pl.ds(key_start(kt), tk)], ones], axis=0)

    acc = _sweep_key_tiles(scores, values, s_sc, p_sc, acc_sc, q0, tq, tk, unroll)
    o = acc[:V_DIM_A] / acc[V_DIM_A:V_DIM_A + 1]
    o_ref[:, :V_DIM_A] = o.T.astype(o_ref.dtype)
    o_ref[:, V_DIM_A:] = jnp.zeros((tq, LANE - V_DIM_A), o_ref.dtype)


def _diff_prompt_body(q_ref, k_ref, vt_ref, slope_ref, lamv_ref, g_ref, o_ref, s_sc, p_sc, acc_sc,
                      *, tq, tk, unroll, lam_init):
    q0 = pl.program_id(1) * tq
    qs = _stack_diff_queries(q_ref[...])
    slope = slope_ref[0][:, :1] * LOG2E
    ones = _ones_row_block(tk)

    def key_start(kt):
        return pl.multiple_of(kt * tk, tk)

    def scores(kt, masked):
        s = _dot_nt(k_ref[pl.ds(key_start(kt), tk), :], qs)
        c_rel = (kt * tk - q0 + lax.broadcasted_iota(I32, (tk, LANE), 0)).astype(F32)
        if not masked:
            return s + jnp.concatenate([slope * c_rel] * (2 * tq // LANE), axis=1)
        strips = []
        for c in range(tq // LANE):
            a = (c * LANE + lax.broadcasted_iota(I32, (tk, LANE), 1)).astype(F32)
            bias = slope * jnp.minimum(c_rel, 2.0 * a - c_rel)
            strips.append(jnp.where(_visible_strip(q0 + c * LANE, kt * tk, tk), bias, NEG))
        return s + jnp.concatenate(strips + strips, axis=1)

    def values(kt):
        return jnp.concatenate([vt_ref[:, pl.ds(key_start(kt), tk)], ones], axis=0)

    acc = _sweep_key_tiles(scores, values, s_sc, p_sc, acc_sc, q0, tq, tk, unroll)
    dv = 2 * DH_B
    o1 = acc[:dv, :tq] / acc[dv:dv + 1, :tq]
    o2 = acc[:dv, tq:] / acc[dv:dv + 1, tq:]
    o = o1 - _diff_lambda(lamv_ref, lam_init) * o2
    o = o * lax.rsqrt(jnp.mean(o * o, axis=0, keepdims=True) + RMS_EPS)
    o_ref[...] = (o.T * g_ref[...] * (1.0 - lam_init)).astype(o_ref.dtype)


def _prompt_attn_call(body, q, k, vt, extra, extra_specs, *, maps, tq, tk, unroll, name):
    t = q.shape[0]
    dv = vt.shape[0] // H_A
    width = maps * tq
    assert t % tq == 0 and t % tk == 0 and (tq % tk == 0 or tk % tq == 0) and tq % LANE == 0 and tk % CHUNK == 0
    qspec = pl.BlockSpec((tq, LANE), lambda h, i: (i, h))
    return pl.pallas_call(
        functools.partial(body, tq=tq, tk=tk, unroll=unroll),
        out_shape=jax.ShapeDtypeStruct((t, HEAD_COLS), BF16),
        grid=(H_A, t // tq),
        in_specs=[qspec, pl.BlockSpec((t, LANE), lambda h, i: (0, h)),
                  pl.BlockSpec((dv, t), lambda h, i: (h, 0))] + extra_specs,
        out_specs=qspec,
        scratch_shapes=[pltpu.VMEM((2, tk, width), F32), pltpu.VMEM((2, tk, width), BF16),
                        pltpu.VMEM((dv + BF16_ROWS, width), F32)],
        compiler_params=_params(("parallel", "arbitrary")),
        name=name,
    )(q, k, vt, *extra)


def _chunk_visible(q0, k0, shape):
    r = q0 + lax.broadcasted_iota(I32, shape, 0)
    c = k0 + lax.broadcasted_iota(I32, shape, 1)
    return (c >> CHUNK_SHIFT) <= (r >> CHUNK_SHIFT)


def _two_part_softmax_pv(s_past, s_new, v_past, v_new):
    m = jnp.maximum(jnp.max(s_past, axis=-1, keepdims=True), jnp.max(s_new, axis=-1, keepdims=True))
    p_past = jnp.exp2(s_past - m)
    p_new = jnp.exp2(s_new - m)
    l = jnp.sum(p_past, axis=-1, keepdims=True) + jnp.sum(p_new, axis=-1, keepdims=True)
    acc = _dot(p_past.astype(BF16), v_past) + _dot(p_new.astype(BF16), v_new)
    return acc / l


def _mla_sample_body(q_ref, kp_ref, vp_ref, kn_ref, vn_ref, o_ref, *, past):
    q = q_ref[...]
    s_past = _dot_nt(q, kp_ref[...])
    s_new = _dot_nt(q, kn_ref[...])
    s_past = jnp.where(_chunk_visible(past, 0, s_past.shape), s_past, NEG)
    s_new = jnp.where(_chunk_visible(past, past, s_new.shape), s_new, NEG)
    o_ref[...] = _two_part_softmax_pv(s_past, s_new, vp_ref[...], vn_ref[...]).astype(o_ref.dtype)


def _diff_sample_body(q_ref, kp_ref, vp_ref, kn_ref, vn_ref, slope_ref, lamv_ref, g_ref, o_ref,
                      *, past, lam_init):
    sq = q_ref.shape[0]
    qs = _stack_diff_queries(q_ref[...])
    slope = slope_ref[0][:, :1] * LOG2E

    def scores(k, k0):
        n = k.shape[0]
        s = _dot_nt(qs, k.astype(BF16))
        qpos = past + lax.broadcasted_iota(I32, (sq, n), 0)
        kpos = k0 + lax.broadcasted_iota(I32, (sq, n), 1)
        bias = -slope * jnp.abs(qpos - kpos).astype(F32)
        vis = _chunk_visible(past, k0, (sq, n))
        bias = jnp.concatenate([bias, bias], axis=0)
        vis = jnp.concatenate([vis, vis], axis=0)
        return jnp.where(vis, s + bias, NEG)

    o = _two_part_softmax_pv(scores(kp_ref[...], 0), scores(kn_ref[...], past),
                             vp_ref[...].astype(BF16), vn_ref[...])
    lam = _diff_lambda(lamv_ref, lam_init)
    o = o[:sq] - lam * o[sq:]
    o_ref[...] = (_rms_norm(o, g_ref[...], RMS_EPS) * (1.0 - lam_init)).astype(o_ref.dtype)


def _sample_attn_call(body, q, kp, vp, kn, vn, extra, extra_specs, *, batch, past, name):
    sq = q.shape[0] // batch
    assert past % CHUNK == 0 and kp.shape[0] == batch * past
    new = pl.BlockSpec((sq, LANE), lambda b, h: (b, h))
    old = pl.BlockSpec((past, LANE), lambda b, h: (b, h))
    return pl.pallas_call(
        functools.partial(body, past=past),
        out_shape=jax.ShapeDtypeStruct((batch * sq, HEAD_COLS), BF16),
        grid=(batch, H_A),
        in_specs=[new, old, old, new, new] + extra_specs,
        out_specs=new,
        compiler_params=_params(("parallel", "parallel")),
        name=name,
    )(q, kp, vp, kn, vn, *extra)


def _top4_of_row(logits):
    lane = lax.broadcasted_iota(I32, logits.shape, 1)
    vals = logits
    top_v, top_i = [], []
    for _ in range(TOP_K):
        mx = jnp.max(vals, axis=-1, keepdims=True)
        sel = jnp.min(jnp.where(vals == mx, lane, N_EXPERTS), axis=-1, keepdims=True)
        top_v.append(mx)
        top_i.append(sel)
        vals = jnp.where(lane == sel, -jnp.inf, vals)
    e = [jnp.exp(v - top_v[0]) for v in top_v]
    tot = e[0] + e[1] + e[2] + e[3]
    return top_i, [x / tot for x in e]


def _outproj_body(oa_ref, ob_ref, sga_ref, sgb_ref, x_ref, woa_ref, wob_ref, wout_ref, g_ref, b_ref,
                  wr_ref, br_ref, x1_ref, x1t_ref, idx_ref, gate_ref):
    a = _dot(oa_ref[...], woa_ref[...])
    b = _dot(ob_ref[...], wob_ref[...])
    merged = sga_ref[...].astype(F32) * a + sgb_ref[...].astype(F32) * b
    y = _dot(merged.astype(BF16), wout_ref[...])
    x1 = _layer_norm(ALPHA * x_ref[...] + y, g_ref[...], b_ref[...])
    x1_ref[...] = x1
    _store_row_tiles(x1t_ref, x1)
    logits = jnp.dot(x1, wr_ref[...], preferred_element_type=F32, precision=lax.Precision.HIGHEST) + br_ref[...]
    top_i, gates = _top4_of_row(logits)
    lane = lax.broadcasted_iota(I32, idx_ref.shape, 1)
    idx = jnp.zeros(idx_ref.shape, I32)
    gate = jnp.zeros(gate_ref.shape, F32)
    for k in range(TOP_K):
        idx = jnp.where(lane == k, top_i[k], idx)
        gate = jnp.where(lane == k, gates[k], gate)
    idx_ref[...] = idx
    gate_ref[...] = gate


def _outproj_call(oa, ob, sga, sgb, x, woa, wob, wout, g, b, wr, br):
    t, d = x.shape
    tm = _pick_tile(t, 256)
    row = lambda n: pl.BlockSpec((tm, n), lambda i: (i, 0))
    const = lambda a: pl.BlockSpec(a.shape, lambda i: (0, 0))
    return pl.pallas_call(
        _outproj_body,
        out_shape=(jax.ShapeDtypeStruct((t, d), F32), jax.ShapeDtypeStruct((t, d // LANE, LANE), F32),
                   jax.ShapeDtypeStruct((t, LANE), I32), jax.ShapeDtypeStruct((t, LANE), F32)),
        grid=(t // tm,),
        in_specs=[row(HEAD_COLS), row(HEAD_COLS), row(HEAD_COLS), row(HEAD_COLS), row(d),
                  const(woa), const(wob), const(wout), const(g), const(b), const(wr), const(br)],
        out_specs=(row(d), pl.BlockSpec((tm, d // LANE, LANE), lambda i: (i, 0, 0)), row(LANE), row(LANE)),
        compiler_params=_params(("parallel",)),
        name="outproj",
    )(oa, ob, sga, sgb, x, woa, wob, wout, g, b, wr, br)


def _store_row_tiles(ref, x):
    for j in range(x.shape[1] // LANE):
        ref[:, j, :] = x[:, j * LANE:(j + 1) * LANE]


def _load_row_tiles(ref):
    return jnp.concatenate([ref[:, j, :] for j in range(ref.shape[1])], axis=1)


def _rank_body(idx_ref, rank_ref, cnt_ref, run_sc):
    tr = idx_ref.shape[0]

    @pl.when(pl.program_id(0) == 0)
    def _():
        run_sc[...] = jnp.zeros(run_sc.shape, F32)

    idx = idx_ref[...]
    lane = lax.broadcasted_iota(I32, (tr, LANE), 1)
    chosen = [idx[:, k:k + 1] == lane for k in range(TOP_K)]
    member = functools.reduce(lambda a, b: a | b, chosen).astype(F32)
    below = (lax.broadcasted_iota(I32, (tr, tr), 0) > lax.broadcasted_iota(I32, (tr, tr), 1)).astype(BF16)
    before = _dot(below, member.astype(BF16)) + run_sc[...]
    rank = jnp.zeros((tr, LANE), F32)
    for k in range(TOP_K):
        rank = jnp.where(lane == k, jnp.sum(jnp.where(chosen[k], before, 0.0), axis=-1, keepdims=True), rank)
    rank_ref[...] = rank.astype(I32)
    run_sc[...] = run_sc[...] + jnp.sum(member, axis=0, keepdims=True)
    cnt_ref[...] = run_sc[...]


def _rank_call(top_idx):
    t = top_idx.shape[0]
    tr = _pick_tile(t, 256)
    row = pl.BlockSpec((tr, LANE), lambda i: (i, 0))
    return pl.pallas_call(
        _rank_body,
        out_shape=(jax.ShapeDtypeStruct((t, LANE), I32), jax.ShapeDtypeStruct((1, LANE), F32)),
        grid=(t // tr,),
        in_specs=[row],
        out_specs=(row, pl.BlockSpec((1, LANE), lambda i: (0, 0))),
        scratch_shapes=[pltpu.VMEM((1, LANE), F32)],
        compiler_params=_params(("arbitrary",)),
        name="moe_rank",
    )(top_idx)


def _route(top_idx, bm):
    t = top_idx.shape[0]
    rank, counts = _rank_call(top_idx)
    counts = counts[0, :N_EXPERTS].astype(I32)
    padded = (counts + bm - 1) // bm * bm
    pad_end = jnp.cumsum(padded)
    pad_start = pad_end - padded
    n_blocks = -(-t * TOP_K // bm) + N_EXPERTS
    block_e = jnp.minimum(
        jnp.sum(pad_end[None, :] <= (jnp.arange(n_blocks, dtype=I32) * bm)[:, None], axis=1), N_EXPERTS - 1).astype(I32)
    n_used = (pad_end[-1:] // bm).astype(I32)
    idx = top_idx[:, :TOP_K]
    start_of = jnp.sum(jnp.where(idx[:, :, None] == jnp.arange(N_EXPERTS, dtype=I32), pad_start, 0), axis=-1)
    return block_e, n_used, (start_of + rank[:, :TOP_K]).astype(I32)


DMA_UNROLL = 16


def _for_row_groups(n, fn):
    assert n % DMA_UNROLL == 0

    def body(g, c):
        for u in range(DMA_UNROLL):
            fn(g * DMA_UNROLL + u, u)
        return c

    lax.fori_loop(0, n // DMA_UNROLL, body, 0)


def _dispatch_body(dest_ref, x_ref, xs_in_hbm, xs_hbm, sem, *, td):
    del xs_in_hbm
    rows = TOP_K * td
    assert DMA_UNROLL % TOP_K == 0

    def start(r, u):
        tok = (r - u) // TOP_K + u // TOP_K
        pltpu.make_async_copy(x_ref.at[tok], xs_hbm.at[dest_ref[0, 0, r]], sem).start(priority=u % 2)

    def wait(r, u):
        pltpu.make_async_copy(x_ref.at[0], xs_hbm.at[0], sem).wait()

    _for_row_groups(rows, start)
    _for_row_groups(rows, wait)


def _dispatch_call(dest, x_tiles, xs_buf, *, td):
    t = x_tiles.shape[0]
    n = t // td
    rows = TOP_K * td
    return pl.pallas_call(
        functools.partial(_dispatch_body, td=td),
        out_shape=jax.ShapeDtypeStruct(xs_buf.shape, F32),
        grid=(n,),
        in_specs=[pl.BlockSpec((1, 1, rows), lambda i: (i, 0, 0), memory_space=pltpu.SMEM),
                  pl.BlockSpec((td,) + x_tiles.shape[1:], lambda i: (i, 0, 0)), pl.BlockSpec(memory_space=pl.ANY)],
        out_specs=pl.BlockSpec(memory_space=pl.ANY),
        scratch_shapes=[pltpu.SemaphoreType.DMA],
        input_output_aliases={2: 0},
        compiler_params=_params(("arbitrary",)),
        name="moe_dispatch",
    )(dest.reshape(n, 1, rows), x_tiles, xs_buf)


def _moe_body(be_ref, nb_ref, xs_ref, wgu_ref, bgu_ref, wdn_ref, bdn_ref, y_ref):
    @pl.when(pl.program_id(0) < nb_ref[0])
    def _():
        x = _load_row_tiles(xs_ref).astype(BF16)
        h = _dot(x, wgu_ref[0]) + bgu_ref[0]
        g = jnp.minimum(h[:, :D_FF], SWIGLU_LIMIT)
        u = jnp.clip(h[:, D_FF:], -SWIGLU_LIMIT, SWIGLU_LIMIT)
        act = (u + 1.0) * (g * jax.nn.sigmoid(SWIGLU_ALPHA * g))
        _store_row_tiles(y_ref, _dot(act.astype(BF16), wdn_ref[0]) + bdn_ref[0])

    @pl.when(pl.program_id(0) >= nb_ref[0])
    def _():
        y_ref[...] = jnp.zeros(y_ref.shape, F32)


def _moe_call(block_e, n_used, xs, wgu, bgu, wdn, bdn, *, bm):
    n_blocks = block_e.shape[0]
    d = D_MODEL
    tiles = pl.BlockSpec((bm, d // LANE, LANE), lambda b, be, nb: (b, 0, 0))
    grid_spec = pltpu.PrefetchScalarGridSpec(
        num_scalar_prefetch=2,
        grid=(n_blocks,),
        in_specs=[
            tiles,
            pl.BlockSpec((1, d, 2 * D_FF), lambda b, be, nb: (be[b], 0, 0)),
            pl.BlockSpec((1, 1, 2 * D_FF), lambda b, be, nb: (be[b], 0, 0)),
            pl.BlockSpec((1, D_FF, d), lambda b, be, nb: (be[b], 0, 0)),
            pl.BlockSpec((1, 1, d), lambda b, be, nb: (be[b], 0, 0)),
        ],
        out_specs=tiles,
    )
    return pl.pallas_call(
        _moe_body,
        out_shape=jax.ShapeDtypeStruct(xs.shape, F32),
        grid_spec=grid_spec,
        compiler_params=_params(("arbitrary",)),
        name="moe_experts",
    )(block_e, n_used, xs, wgu, bgu.reshape(N_EXPERTS, 1, 2 * D_FF), wdn, bdn.reshape(N_EXPERTS, 1, d))


def _start_row_gather(idx_ref, n, src_hbm, dst, sem):
    _for_row_groups(n, lambda r, u: pltpu.make_async_copy(
        src_hbm.at[idx_ref[0, 0, r]], dst.at[r], sem).start(priority=u % 2))


def _wait_row_gather(n, src_hbm, dst, sem):
    _for_row_groups(n, lambda r, u: pltpu.make_async_copy(src_hbm.at[0], dst.at[r], sem).wait())


def _combine_body(dest_ref, destn_ref, y_hbm, x_ref, gate_ref, g_ref, b_ref, o_ref, ybuf, sem, *, tc):
    i = pl.program_id(0)
    n = pl.num_programs(0)
    slot = i % 2
    rows = TOP_K * tc

    @pl.when(i == 0)
    def _():
        _start_row_gather(dest_ref, rows, y_hbm, ybuf.at[0], sem.at[0])

    @pl.when(i + 1 < n)
    def _():
        _start_row_gather(destn_ref, rows, y_hbm, ybuf.at[1 - slot], sem.at[1 - slot])

    _wait_row_gather(rows, y_hbm, ybuf.at[slot], sem.at[slot])
    gate = gate_ref[...]
    moe = jnp.zeros(x_ref.shape, F32)
    for k in range(TOP_K):
        moe = moe + gate[:, k:k + 1] * _load_row_tiles(ybuf.at[slot, k * tc:(k + 1) * tc])
    o_ref[...] = _layer_norm(ALPHA * x_ref[...] + moe, g_ref[...], b_ref[...])


def _combine_call(dest, y, x, gate, g, b, *, tc):
    t, d = x.shape
    n = t // tc
    rows = TOP_K * tc
    dest3 = dest.reshape(n, tc, TOP_K).transpose(0, 2, 1).reshape(n, 1, rows)
    smem_row = lambda f: pl.BlockSpec((1, 1, rows), f, memory_space=pltpu.SMEM)
    row = lambda w: pl.BlockSpec((tc, w), lambda i: (i, 0))
    const = lambda a: pl.BlockSpec(a.shape, lambda i: (0, 0))
    return pl.pallas_call(
        functools.partial(_combine_body, tc=tc),
        out_shape=jax.ShapeDtypeStruct((t, d), F32),
        grid=(n,),
        in_specs=[smem_row(lambda i: (i, 0, 0)), smem_row(lambda i: (jnp.minimum(i + 1, n - 1), 0, 0)),
                  pl.BlockSpec(memory_space=pl.ANY), row(d), row(LANE), const(g), const(b)],
        out_specs=row(d),
        scratch_shapes=[pltpu.VMEM((2, rows, d // LANE, LANE), F32), pltpu.SemaphoreType.DMA((2,))],
        compiler_params=_params(("arbitrary",)),
        name="moe_combine",
    )(dest3, dest3, y, x, gate, g, b)


def _head_blocks(w, widths):
    per = sum(widths)
    w = w.reshape(*w.shape[:-1], H_A, per)
    pad = [(0, 0)] * (w.ndim - 1) + [(0, LANE - per)]
    return jnp.pad(w, pad).reshape(*w.shape[:-2], HEAD_COLS)


def _rotate_half_cols(w):
    half = ROPE_DIM // 2
    return jnp.concatenate([-w[..., half:], w[..., :half]], axis=-1)


def _prep_layer(w_in, w_uq, w_ukv, w_oa):
    d = w_in.shape[0]
    splits = (Q_LORA, KV_LORA, ROPE_DIM, 1024, 1024, 1024, D_MODEL, D_MODEL)
    offs = [0]
    for s in splits:
        offs.append(offs[-1] + s)
    lat = w_in[:, offs[0]:offs[2]]
    kr = w_in[:, offs[2]:offs[3]]
    rest = w_in[:, offs[3]:]

    def rope_block(w):
        return jnp.pad(w, ((0, 0), (NOPE_DIM, LANE - NOPE_DIM - ROPE_DIM)))

    w1 = jnp.concatenate([lat, rope_block(kr), rope_block(_rotate_half_cols(kr)), rest], axis=1).astype(BF16)
    assert w1.shape == (d, C_END)
    wvt = w_in[:, offs[5]:offs[6]].T.astype(BF16)

    uq = w_uq.reshape(Q_LORA, H_A, NOPE_DIM + ROPE_DIM)
    uq_rot = jnp.concatenate([jnp.zeros((Q_LORA, H_A, NOPE_DIM), F32), _rotate_half_cols(uq[..., NOPE_DIM:])], -1)
    wuq = jnp.concatenate([_head_blocks(uq.reshape(Q_LORA, -1), (NOPE_DIM + ROPE_DIM,)),
                           _head_blocks(uq_rot.reshape(Q_LORA, -1), (NOPE_DIM + ROPE_DIM,))], axis=1).astype(BF16)

    ukv = w_ukv.reshape(KV_LORA, H_A, NOPE_DIM + V_DIM_A)
    wuk = _head_blocks(ukv[..., :NOPE_DIM].reshape(KV_LORA, -1), (NOPE_DIM,)).astype(BF16)
    wuv = _head_blocks(ukv[..., NOPE_DIM:].reshape(KV_LORA, -1), (V_DIM_A,)).astype(BF16)
    wuvt = ukv[..., NOPE_DIM:].reshape(KV_LORA, H_A * V_DIM_A).T.astype(BF16)

    woa = jnp.pad(w_oa.reshape(H_A, V_DIM_A, D_MODEL), ((0, 0), (0, LANE - V_DIM_A), (0, 0)))
    woa = woa.reshape(HEAD_COLS, D_MODEL).astype(BF16)
    return w1, wvt, wuq, wuk, wuv, wuvt, woa


def _rope_tables(pos):
    half = ROPE_DIM // 2
    inv = ROPE_THETA ** (-jnp.arange(half, dtype=F32) / half)
    ang = pos.astype(F32)[:, None] * inv[None, :]
    cos, sin = jnp.cos(ang), jnp.sin(ang)
    n = pos.shape[0]
    tail = jnp.zeros((n, LANE - NOPE_DIM - ROPE_DIM), F32)
    cos_t = jnp.concatenate([jnp.ones((n, NOPE_DIM), F32), cos, cos, tail], axis=1)
    sin_t = jnp.concatenate([jnp.zeros((n, NOPE_DIM), F32), sin, sin, tail], axis=1)
    return cos_t, sin_t


def _rope_block_of(kr):
    return jnp.pad(kr, ((0, 0), (NOPE_DIM, LANE - NOPE_DIM - ROPE_DIM)))


def _moe_block_rows(t):
    return 256 if t * TOP_K >= 256 * N_EXPERTS else 128


def _moe_row_buffer(t):
    bm = _moe_block_rows(t)
    return jnp.zeros(((-(-t * TOP_K // bm) + N_EXPERTS) * bm, D_MODEL // LANE, LANE), F32)


def _moe_layer(x1, x1_tiles, top_idx, gate, rows_buf, wgu, bgu, wdn, bdn, g, b):
    t = x1.shape[0]
    bm = _moe_block_rows(t)
    tc = _pick_tile(t, 128)
    block_e, n_used, dest = _route(top_idx, bm)
    xs = _dispatch_call(dest, x1_tiles, rows_buf, td=_pick_tile(t, 512))
    y = _moe_call(block_e, n_used, xs, wgu, bgu, wdn, bdn, bm=bm)
    return _combine_call(dest, y, x1, gate, g, b, tc=tc), y


def kernel(x_prompt, x_sample, cache_mla_latent, cache_mla_krope, cache_diff_k, cache_diff_v,
           ln_in_g, ln_in_b, w_in, q_norm_g, kv_norm_g, w_uq, w_ukv, lam_q1, lam_k1, lam_q2, lam_k2,
           subln_g, w_oa, w_ob, w_out, ln1_g, ln1_b, w_router, b_router, w_gu, b_gu, w_dn, b_dn,
           ln2_g, ln2_b):
    bp, sp, d = x_prompt.shape
    bs, ss, _ = x_sample.shape
    past = cache_mla_latent.shape[2]
    assert bp == 1 and d == D_MODEL
    tp, ts = bp * sp, bs * ss

    cos_p, sin_p = _rope_tables(jnp.arange(sp, dtype=I32))
    cos_s, sin_s = _rope_tables(jnp.tile(past + jnp.arange(ss, dtype=I32), bs))
    slopes = 2.0 ** (-8.0 * (jnp.arange(H_B, dtype=F32) + 1.0) / H_B)
    slopes = jnp.broadcast_to(slopes[:, None, None], (H_B, 1, LANE))
    tk = _pick_tile(sp, 256)
    tq_mla = _pick_tile(sp, 512)
    tq_diff = _pick_tile(sp, 256)

    xp = _ln_call(x_prompt.reshape(tp, d), ln_in_g, ln_in_b)
    xs = _ln_call(x_sample.reshape(ts, d), ln_in_g, ln_in_b)
    rows_p, rows_s = _moe_row_buffer(tp), _moe_row_buffer(ts)
    new_p = ([], [], [], [])
    new_s = ([], [], [], [])
    for l in range(DEPTH):
        lam_init = 0.8 - 0.6 * math.exp(-0.3 * l)
        w1, wvt, wuq, wuk, wuv, wuvt, woa = _prep_layer(w_in[l], w_uq[l], w_ukv[l], w_oa[l])
        gq = q_norm_g[l].reshape(1, Q_LORA)
        gkv = kv_norm_g[l].reshape(1, KV_LORA)
        lamv = jnp.stack([lam_q1[l], lam_k1[l], lam_q2[l], lam_k2[l]])
        subg = subln_g[l].reshape(1, 2 * DH_B)
        wob = w_ob[l].astype(BF16)
        wout = w_out[l].astype(BF16)
        g1, b1 = ln1_g[l].reshape(1, d), ln1_b[l].reshape(1, d)
        g2, b2 = ln2_g[l].reshape(1, d), ln2_b[l].reshape(1, d)
        wr, br = w_router[l], b_router[l].reshape(1, N_EXPERTS)
        wgu, wdn = w_gu[l].astype(BF16), w_dn[l].astype(BF16)
        diff_extra = [slopes, lamv, subg]

        def diff_specs(head_of):
            return [pl.BlockSpec((1, 1, LANE), lambda a, b: (head_of(a, b), 0, 0)),
                    pl.BlockSpec(lamv.shape, lambda a, b: (0, 0)), pl.BlockSpec(subg.shape, lambda a, b: (0, 0))]

        qa, lat, krb, qd, kd32, kd16, vd32, _, vdt, sga, sgb = _inproj_call(xp, w1, wvt, wuq, gq, gkv, cos_p, sin_p)
        ka, vat = _kvup_call(lat, krb, wuk, wuvt, feature_major_v=True)
        oa = _prompt_attn_call(_mla_prompt_body, qa, ka, vat, [], [], maps=1, tq=tq_mla, tk=tk, unroll=MLA_UNROLL,
                               name="mla_prompt")
        ob = _prompt_attn_call(functools.partial(_diff_prompt_body, lam_init=lam_init), qd, kd16, vdt,
                               diff_extra, diff_specs(lambda h, i: h), maps=2, tq=tq_diff, tk=tk, unroll=DIFF_UNROLL,
                               name="diff_prompt")
        xp1, xp1t, idx_p, gate_p = _outproj_call(oa, ob, sga, sgb, xp, woa, wob, wout, g1, b1, wr, br)
        new_p[0].append(lat.reshape(bp, sp, KV_LORA))
        new_p[1].append(krb[:, NOPE_DIM:NOPE_DIM + ROPE_DIM].reshape(bp, sp, ROPE_DIM))
        new_p[2].append(kd32.reshape(bp, sp, 2 * H_B, DH_B))
        new_p[3].append(vd32.reshape(bp, sp, H_B, 2 * DH_B))

        qa, lat, krb, qd, kd32, kd16, vd32, vd16, _, sga, sgb = _inproj_call(xs, w1, wvt, wuq, gq, gkv, cos_s, sin_s)
        kn, vn = _kvup_call(lat, krb, wuk, wuv, feature_major_v=False)
        kpast, vpast = _kvup_call(cache_mla_latent[l].reshape(bs * past, KV_LORA),
                                  _rope_block_of(cache_mla_krope[l].reshape(bs * past, ROPE_DIM)), wuk, wuv,
                                  feature_major_v=False)
        oa = _sample_attn_call(_mla_sample_body, qa, kpast, vpast, kn, vn, [], [], batch=bs, past=past,
                               name="mla_sample")
        ob = _sample_attn_call(functools.partial(_diff_sample_body, lam_init=lam_init), qd,
                               cache_diff_k[l].reshape(bs * past, HEAD_COLS),
                               cache_diff_v[l].reshape(bs * past, HEAD_COLS), kd16, vd16,
                               diff_extra, diff_specs(lambda b, h: h), batch=bs, past=past, name="diff_sample")
        xs1, xs1t, idx_s, gate_s = _outproj_call(oa, ob, sga, sgb, xs, woa, wob, wout, g1, b1, wr, br)
        new_s[0].append(lat.reshape(bs, ss, KV_LORA))
        new_s[1].append(krb[:, NOPE_DIM:NOPE_DIM + ROPE_DIM].reshape(bs, ss, ROPE_DIM))
        new_s[2].append(kd32.reshape(bs, ss, 2 * H_B, DH_B))
        new_s[3].append(vd32.reshape(bs, ss, H_B, 2 * DH_B))

        xp, rows_p = _moe_layer(xp1, xp1t, idx_p, gate_p, rows_p, wgu, b_gu[l], wdn, b_dn[l], g2, b2)
        xs, rows_s = _moe_layer(xs1, xs1t, idx_s, gate_s, rows_s, wgu, b_gu[l], wdn, b_dn[l], g2, b2)

    return (xp.reshape(bp, sp, d), xs.reshape(bs, ss, d),
            jnp.stack(new_p[0]), jnp.stack(new_p[1]), jnp.stack(new_p[2]), jnp.stack(new_p[3]),
            jnp.stack(new_s[0]), jnp.stack(new_s[1]), jnp.stack(new_s[2]), jnp.stack(new_s[3]))
```

```python
import functools
import math

import jax
import jax.numpy as jnp
from jax import lax
from jax.experimental import pallas as pl
from jax.experimental.pallas import tpu as pltpu

F32 = jnp.float32
BF16 = jnp.bfloat16
I32 = jnp.int32

D_MODEL = 1024
DEPTH = 4
CHUNK = 64
CHUNK_SHIFT = 6
H_A = 8
NOPE_DIM = 64
ROPE_DIM = 32
V_DIM_A = 64
Q_LORA = 256
KV_LORA = 256
ROPE_THETA = 10000.0
H_B = 8
DH_B = 64
N_EXPERTS = 32
TOP_K = 4
D_FF = 1024
SWIGLU_LIMIT = 7.0
SWIGLU_ALPHA = 1.702
LN_EPS = 1e-5
RMS_EPS = 1e-6
ALPHA = (2 * DEPTH) ** 0.25

LANE = 128
BF16_ROWS = 16
HEAD_COLS = H_A * LANE
LOG2E = 1.4426950408889634
MLA_QSCALE = (NOPE_DIM + ROPE_DIM) ** -0.5 * LOG2E
DIFF_QSCALE = DH_B ** -0.5 * LOG2E
MLA_UNROLL = 8
DIFF_UNROLL = 8
NEG = -1e30
VMEM_LIMIT = 48 * 1024 * 1024

C_LAT = 0
C_KR = 512
C_KRROT = 640
C_QD = 768
C_KD = C_QD + 1024
C_VD = C_KD + 1024
C_GA = C_VD + 1024
C_GB = C_GA + 1024
C_END = C_GB + 1024


def _pick_tile(n, pref):
    t = min(n, pref)
    while n % t:
        t -= 8
    assert t > 0 and (t % 8 == 0 or t == n), (n, pref)
    return t


def _params(sem):
    return pltpu.CompilerParams(dimension_semantics=sem, vmem_limit_bytes=VMEM_LIMIT)


def _layer_norm(x, g, b):
    mu = jnp.mean(x, axis=-1, keepdims=True)
    xc = x - mu
    var = jnp.mean(xc * xc, axis=-1, keepdims=True)
    return xc * lax.rsqrt(var + LN_EPS) * g + b


def _rms_norm(x, g, eps):
    ms = jnp.mean(x * x, axis=-1, keepdims=True)
    return x * lax.rsqrt(ms + eps) * g


def _dot(a, b):
    return jnp.dot(a, b, preferred_element_type=F32)


def _dot_nt(a, b):
    return lax.dot_general(a, b, (((1,), (1,)), ((), ())), preferred_element_type=F32)


def _ln_body(x_ref, g_ref, b_ref, o_ref):
    o_ref[...] = _layer_norm(x_ref[...], g_ref[...], b_ref[...])


def _ln_call(x, g, b):
    t, d = x.shape
    tm = _pick_tile(t, 512)
    row = pl.BlockSpec((tm, d), lambda i: (i, 0))
    vec = pl.BlockSpec((1, d), lambda i: (0, 0))
    return pl.pallas_call(
        _ln_body,
        out_shape=jax.ShapeDtypeStruct((t, d), F32),
        grid=(t // tm,),
        in_specs=[row, vec, vec],
        out_specs=row,
        compiler_params=_params(("parallel",)),
        name="ln_in",
    )(x, g.reshape(1, d), b.reshape(1, d))


def _inproj_body(x_ref, w_ref, wvt_ref, wuq_ref, gq_ref, gkv_ref, cos_ref, sin_ref,
                 q_ref, lat_ref, krb_ref, qd_ref, kd32_ref, kd16_ref, vd32_ref, vd16_ref, vdt_ref,
                 sga_ref, sgb_ref):
    xb = x_ref[...].astype(BF16)
    cos = cos_ref[...]
    sin = sin_ref[...]

    def proj(lo, hi):
        return _dot(xb, w_ref[:, lo:hi])

    z = proj(C_LAT, C_QD)
    cq = _rms_norm(z[:, :Q_LORA], gq_ref[...], RMS_EPS)
    lat_ref[...] = _rms_norm(z[:, Q_LORA:C_KR], gkv_ref[...], RMS_EPS)
    krb_ref[...] = z[:, C_KR:C_KRROT] * cos + z[:, C_KRROT:C_QD] * sin
    qq = _dot(cq.astype(BF16), wuq_ref[...])
    for h in range(H_A):
        lo = h * LANE
        qh = qq[:, lo:lo + LANE] * cos + qq[:, HEAD_COLS + lo:HEAD_COLS + lo + LANE] * sin
        q_ref[:, lo:lo + LANE] = (qh * MLA_QSCALE).astype(BF16)
    qd_ref[...] = (proj(C_QD, C_KD) * DIFF_QSCALE).astype(BF16)
    z = proj(C_KD, C_VD)
    kd32_ref[...] = z
    kd16_ref[...] = z.astype(BF16)
    z = proj(C_VD, C_GA)
    vd32_ref[...] = z
    vd16_ref[...] = z.astype(BF16)
    vdt_ref[...] = _dot_nt(wvt_ref[...], xb).astype(BF16)
    sga_ref[...] = jax.nn.sigmoid(proj(C_GA, C_GB)).astype(BF16)
    sgb_ref[...] = jax.nn.sigmoid(proj(C_GB, C_END)).astype(BF16)


def _inproj_call(x, w, wvt, wuq, gq, gkv, cos_t, sin_t):
    t, d = x.shape
    tm = _pick_tile(t, 256)
    row = lambda n: pl.BlockSpec((tm, n), lambda i: (i, 0))
    const = lambda a: pl.BlockSpec(a.shape, lambda i: (0, 0), pipeline_mode=pl.Buffered(1))
    wide = lambda dt: jax.ShapeDtypeStruct((t, HEAD_COLS), dt)
    return pl.pallas_call(
        _inproj_body,
        out_shape=(wide(BF16), jax.ShapeDtypeStruct((t, KV_LORA), F32), jax.ShapeDtypeStruct((t, LANE), F32),
                   wide(BF16), wide(F32), wide(BF16), wide(F32), wide(BF16),
                   jax.ShapeDtypeStruct((HEAD_COLS, t), BF16), wide(BF16), wide(BF16)),
        grid=(t // tm,),
        in_specs=[row(d), const(w), const(wvt), const(wuq), const(gq), const(gkv), row(LANE), row(LANE)],
        out_specs=(row(HEAD_COLS), row(KV_LORA), row(LANE), row(HEAD_COLS), row(HEAD_COLS), row(HEAD_COLS),
                   row(HEAD_COLS), row(HEAD_COLS), pl.BlockSpec((HEAD_COLS, tm), lambda i: (0, i)),
                   row(HEAD_COLS), row(HEAD_COLS)),
        compiler_params=_params(("parallel",)),
        name="inproj",
    )(x, w, wvt, wuq, gq, gkv, cos_t, sin_t)


def _kvup_body(lat_ref, krb_ref, wk_ref, wv_ref, k_ref, v_ref, *, feature_major_v):
    lat = lat_ref[...].astype(BF16)
    kn = _dot(lat, wk_ref[...])
    krb = krb_ref[...]
    for h in range(H_A):
        lo = h * LANE
        k_ref[:, lo:lo + LANE] = (kn[:, lo:lo + LANE] + krb).astype(BF16)
    if feature_major_v:
        v_ref[...] = _dot_nt(wv_ref[...], lat).astype(BF16)
    else:
        v_ref[...] = _dot(lat, wv_ref[...]).astype(BF16)


def _kvup_call(lat, krb, wk, wv, *, feature_major_v):
    t = lat.shape[0]
    tm = _pick_tile(t, 512)
    row = lambda n: pl.BlockSpec((tm, n), lambda i: (i, 0))
    const = lambda a: pl.BlockSpec(a.shape, lambda i: (0, 0))
    if feature_major_v:
        v_shape, v_spec = (H_A * V_DIM_A, t), pl.BlockSpec((H_A * V_DIM_A, tm), lambda i: (0, i))
    else:
        v_shape, v_spec = (t, HEAD_COLS), row(HEAD_COLS)
    return pl.pallas_call(
        functools.partial(_kvup_body, feature_major_v=feature_major_v),
        out_shape=(jax.ShapeDtypeStruct((t, HEAD_COLS), BF16), jax.ShapeDtypeStruct(v_shape, BF16)),
        grid=(t // tm,),
        in_specs=[row(KV_LORA), row(LANE), const(wk), const(wv)],
        out_specs=(row(HEAD_COLS), v_spec),
        compiler_params=_params(("parallel",)),
        name="kvup",
    )(lat, krb, wk, wv)


def _stack_diff_queries(q):
    lane = lax.broadcasted_iota(I32, q.shape, 1)
    zero = jnp.zeros_like(q)
    return jnp.concatenate([jnp.where(lane < DH_B, q, zero), jnp.where(lane >= DH_B, q, zero)], axis=0)


def _diff_lambda(lamv_ref, lam_init):
    lv = lamv_ref[...]
    a = jnp.sum(lv[0:1] * lv[1:2], axis=-1, keepdims=True)
    b = jnp.sum(lv[2:3] * lv[3:4], axis=-1, keepdims=True)
    return jnp.exp(a) - jnp.exp(b) + lam_init


def _ones_row_block(tk):
    return (lax.broadcasted_iota(I32, (BF16_ROWS, tk), 0) == 0).astype(BF16)


def _visible_strip(q0, k0, tk):
    c = k0 + lax.broadcasted_iota(I32, (tk, LANE), 0)
    r = q0 + lax.broadcasted_iota(I32, (tk, LANE), 1)
    return (c >> CHUNK_SHIFT) <= (r >> CHUNK_SHIFT)


def _weights_and_rescale(s_ref, p_ref, m_prev, mx, offset):
    m_new = jnp.maximum(m_prev, mx + offset)
    p_ref[...] = jnp.exp2(s_ref[...] - (m_new - offset)).astype(BF16)
    return m_new, jnp.exp2(m_prev - m_new)


def _sweep_key_tiles(scores, values, offset, s_sc, p_sc, acc_sc, q0, tq, tk, unroll):
    width = s_sc.shape[2]
    acc_sc[...] = jnp.zeros(acc_sc.shape, F32)
    p_sc[1] = jnp.zeros(p_sc.shape[1:], BF16)
    n_full = q0 // tk
    s0 = scores(0, False)
    s_sc[0] = s0

    def step(j, cur, carry):
        m_prev, alpha_prev, mx = carry
        s_next = scores(j + 1, False)
        s_sc[1 - cur] = s_next
        acc_sc[...] = alpha_prev * acc_sc[...] + _dot(values(jnp.maximum(j - 1, 0)), p_sc[1 - cur])
        m_new, alpha = _weights_and_rescale(s_sc.at[cur], p_sc.at[cur], m_prev, mx, offset(j))
        return m_new, alpha, jnp.max(s_next, axis=0, keepdims=True)

    def steps(n, base, carry):
        for u in range(n):
            carry = step(base + u, u % 2, carry)
        return carry

    carry = (jnp.full((1, width), NEG, F32), jnp.ones((1, width), F32), jnp.max(s0, axis=0, keepdims=True))
    carry = lax.fori_loop(0, n_full // unroll, lambda jj, c: steps(unroll, unroll * jj, c), carry)
    n = unroll // 2
    while n:
        base = n_full // (2 * n) * (2 * n)
        carry = lax.fori_loop(0, (n_full // n) % 2, lambda _, c, n=n, base=base: steps(n, base, c), carry)
        n //= 2
    m, alpha, _ = carry
    last = (n_full + 1) % 2
    masked = [scores(n_full + j, True) for j in range(max(1, tq // tk))]
    acc_sc[...] = alpha * acc_sc[...] + _dot(values(jnp.maximum(n_full - 1, 0)), p_sc[last])
    for j, s in enumerate(masked):
        m_new = jnp.maximum(m, jnp.max(s, axis=0, keepdims=True))
        p = jnp.exp2(s - m_new).astype(BF16)
        acc_sc[...] = jnp.exp2(m - m_new) * acc_sc[...] + _dot(values(n_full + j), p)
        m = m_new
    return acc_sc[...]


def _mla_prompt_body(q_ref, k_ref, vt_ref, o_ref, s_sc, p_sc, acc_sc, *, tq, tk, unroll):
    q0 = pl.program_id(1) * tq
    q = q_ref[...]
    ones = _ones_row_block(tk)

    def key_start(kt):
        return pl.multiple_of(kt * tk, tk)

    def scores(kt, masked):
        s = _dot_nt(k_ref[pl.ds(key_start(kt), tk), :], q)
        if masked:
            vis = jnp.concatenate([_visible_strip(q0 + c * LANE, kt * tk, tk) for c in range(tq // LANE)], axis=1)
            s = jnp.where(vis, s, NEG)
        return s

    def values(kt):
        return jnp.concatenate([vt_ref[:, pl.ds(key_start(kt), tk)], ones], axis=0)

    acc = _sweep_key_tiles(scores, values, lambda kt: 0.0, s_sc, p_sc, acc_sc, q0, tq, tk, unroll)
    o = acc[:V_DIM_A] / acc[V_DIM_A:V_DIM_A + 1]
    o_ref[:, :V_DIM_A] = o.T.astype(o_ref.dtype)
    o_ref[:, V_DIM_A:] = jnp.zeros((tq, LANE - V_DIM_A), o_ref.dtype)


def _diff_prompt_body(q_ref, k_ref, vt_ref, slope_ref, lamv_ref, g_ref, o_ref, s_sc, p_sc, acc_sc,
                      *, tq, tk, unroll, lam_init):
    q0 = pl.program_id(1) * tq
    qs = _stack_diff_queries(q_ref[...])
    slope = slope_ref[0][:, :1] * LOG2E
    ones = _ones_row_block(tk)

    def key_start(kt):
        return pl.multiple_of(kt * tk, tk)

    row_bias = slope * lax.broadcasted_iota(I32, (tk, LANE), 0).astype(F32)

    def offset(kt):
        return slope * (kt * tk - q0).astype(F32)

    def scores(kt, masked):
        s = _dot_nt(k_ref[pl.ds(key_start(kt), tk), :], qs)
        if not masked:
            return s + jnp.concatenate([row_bias] * (2 * tq // LANE), axis=1)
        c_rel = (kt * tk - q0 + lax.broadcasted_iota(I32, (tk, LANE), 0)).astype(F32)
        strips = []
        for c in range(tq // LANE):
            a = (c * LANE + lax.broadcasted_iota(I32, (tk, LANE), 1)).astype(F32)
            bias = slope * jnp.minimum(c_rel, 2.0 * a - c_rel)
            strips.append(jnp.where(_visible_strip(q0 + c * LANE, kt * tk, tk), bias, NEG))
        return s + jnp.concatenate(strips + strips, axis=1)

    def values(kt):
        return jnp.concatenate([vt_ref[:, pl.ds(key_start(kt), tk)], ones], axis=0)

    acc = _sweep_key_tiles(scores, values, offset, s_sc, p_sc, acc_sc, q0, tq, tk, unroll)
    dv = 2 * DH_B
    o1 = acc[:dv, :tq] / acc[dv:dv + 1, :tq]
    o2 = acc[:dv, tq:] / acc[dv:dv + 1, tq:]
    o = o1 - _diff_lambda(lamv_ref, lam_init) * o2
    o = o * lax.rsqrt(jnp.mean(o * o, axis=0, keepdims=True) + RMS_EPS)
    o_ref[...] = (o.T * g_ref[...] * (1.0 - lam_init)).astype(o_ref.dtype)


def _prompt_attn_call(body, q, k, vt, extra, extra_specs, *, maps, tq, tk, unroll, name):
    t = q.shape[0]
    dv = vt.shape[0] // H_A
    width = maps * tq
    assert t % tq == 0 and t % tk == 0 and (tq % tk == 0 or tk % tq == 0) and tq % LANE == 0 and tk % CHUNK == 0
    qspec = pl.BlockSpec((tq, LANE), lambda h, i: (i, h))
    return pl.pallas_call(
        functools.partial(body, tq=tq, tk=tk, unroll=unroll),
        out_shape=jax.ShapeDtypeStruct((t, HEAD_COLS), BF16),
        grid=(H_A, t // tq),
        in_specs=[qspec, pl.BlockSpec((t, LANE), lambda h, i: (0, h)),
                  pl.BlockSpec((dv, t), lambda h, i: (h, 0))] + extra_specs,
        out_specs=qspec,
        scratch_shapes=[pltpu.VMEM((2, tk, width), F32), pltpu.VMEM((2, tk, width), BF16),
                        pltpu.VMEM((dv + BF16_ROWS, width), F32)],
        compiler_params=_params(("parallel", "arbitrary")),
        name=name,
    )(q, k, vt, *extra)


def _chunk_visible(q0, k0, shape):
    r = q0 + lax.broadcasted_iota(I32, shape, 0)
    c = k0 + lax.broadcasted_iota(I32, shape, 1)
    return (c >> CHUNK_SHIFT) <= (r >> CHUNK_SHIFT)


def _two_part_softmax_pv(s_past, s_new, v_past, v_new):
    m = jnp.maximum(jnp.max(s_past, axis=-1, keepdims=True), jnp.max(s_new, axis=-1, keepdims=True))
    p_past = jnp.exp2(s_past - m)
    p_new = jnp.exp2(s_new - m)
    l = jnp.sum(p_past, axis=-1, keepdims=True) + jnp.sum(p_new, axis=-1, keepdims=True)
    acc = _dot(p_past.astype(BF16), v_past) + _dot(p_new.astype(BF16), v_new)
    return acc / l


def _mla_sample_body(q_ref, kp_ref, vp_ref, kn_ref, vn_ref, o_ref, *, past):
    q = q_ref[...]
    s_past = _dot_nt(q, kp_ref[...])
    s_new = _dot_nt(q, kn_ref[...])
    s_past = jnp.where(_chunk_visible(past, 0, s_past.shape), s_past, NEG)
    s_new = jnp.where(_chunk_visible(past, past, s_new.shape), s_new, NEG)
    o_ref[...] = _two_part_softmax_pv(s_past, s_new, vp_ref[...], vn_ref[...]).astype(o_ref.dtype)


def _diff_sample_body(q_ref, kp_ref, vp_ref, kn_ref, vn_ref, slope_ref, lamv_ref, g_ref, o_ref,
                      *, past, lam_init):
    sq = q_ref.shape[0]
    qs = _stack_diff_queries(q_ref[...])
    slope = slope_ref[0][:, :1] * LOG2E

    def scores(k, k0):
        n = k.shape[0]
        s = _dot_nt(qs, k.astype(BF16))
        qpos = past + lax.broadcasted_iota(I32, (sq, n), 0)
        kpos = k0 + lax.broadcasted_iota(I32, (sq, n), 1)
        bias = -slope * jnp.abs(qpos - kpos).astype(F32)
        vis = _chunk_visible(past, k0, (sq, n))
        bias = jnp.concatenate([bias, bias], axis=0)
        vis = jnp.concatenate([vis, vis], axis=0)
        return jnp.where(vis, s + bias, NEG)

    o = _two_part_softmax_pv(scores(kp_ref[...], 0), scores(kn_ref[...], past),
                             vp_ref[...].astype(BF16), vn_ref[...])
    lam = _diff_lambda(lamv_ref, lam_init)
    o = o[:sq] - lam * o[sq:]
    o_ref[...] = (_rms_norm(o, g_ref[...], RMS_EPS) * (1.0 - lam_init)).astype(o_ref.dtype)


def _sample_attn_call(body, q, kp, vp, kn, vn, extra, extra_specs, *, batch, past, name):
    sq = q.shape[0] // batch
    assert past % CHUNK == 0 and kp.shape[0] == batch * past
    new = pl.BlockSpec((sq, LANE), lambda b, h: (b, h))
    old = pl.BlockSpec((past, LANE), lambda b, h: (b, h))
    return pl.pallas_call(
        functools.partial(body, past=past),
        out_shape=jax.ShapeDtypeStruct((batch * sq, HEAD_COLS), BF16),
        grid=(batch, H_A),
        in_specs=[new, old, old, new, new] + extra_specs,
        out_specs=new,
        compiler_params=_params(("parallel", "parallel")),
        name=name,
    )(q, kp, vp, kn, vn, *extra)


def _top4_of_row(logits):
    lane = lax.broadcasted_iota(I32, logits.shape, 1)
    vals = logits
    top_v, top_i = [], []
    for _ in range(TOP_K):
        mx = jnp.max(vals, axis=-1, keepdims=True)
        sel = jnp.min(jnp.where(vals == mx, lane, N_EXPERTS), axis=-1, keepdims=True)
        top_v.append(mx)
        top_i.append(sel)
        vals = jnp.where(lane == sel, -jnp.inf, vals)
    e = [jnp.exp(v - top_v[0]) for v in top_v]
    tot = e[0] + e[1] + e[2] + e[3]
    return top_i, [x / tot for x in e]


def _outproj_body(oa_ref, ob_ref, sga_ref, sgb_ref, x_ref, woa_ref, wob_ref, wout_ref, g_ref, b_ref,
                  wr_ref, br_ref, x1_ref, x1t_ref, idx_ref, gate_ref):
    a = _dot(oa_ref[...], woa_ref[...])
    b = _dot(ob_ref[...], wob_ref[...])
    merged = sga_ref[...].astype(F32) * a + sgb_ref[...].astype(F32) * b
    y = _dot(merged.astype(BF16), wout_ref[...])
    x1 = _layer_norm(ALPHA * x_ref[...] + y, g_ref[...], b_ref[...])
    x1_ref[...] = x1
    _store_row_tiles(x1t_ref, x1)
    x_hi = x1.astype(BF16)
    x_lo = (x1 - x_hi.astype(F32)).astype(BF16)
    hh_hl = _dot(x_hi, wr_ref[...])
    logits = (hh_hl[:, :N_EXPERTS] + hh_hl[:, N_EXPERTS:] + _dot(x_lo, wr_ref[:, :N_EXPERTS])) + br_ref[...]
    top_i, gates = _top4_of_row(logits)
    lane = lax.broadcasted_iota(I32, idx_ref.shape, 1)
    idx = jnp.zeros(idx_ref.shape, I32)
    gate = jnp.zeros(gate_ref.shape, F32)
    for k in range(TOP_K):
        idx = jnp.where(lane == k, top_i[k], idx)
        gate = jnp.where(lane == k, gates[k], gate)
    idx_ref[...] = idx
    gate_ref[...] = gate


def _outproj_call(oa, ob, sga, sgb, x, woa, wob, wout, g, b, wr, br):
    t, d = x.shape
    tm = _pick_tile(t, 256)
    row = lambda n: pl.BlockSpec((tm, n), lambda i: (i, 0))
    const = lambda a: pl.BlockSpec(a.shape, lambda i: (0, 0))
    return pl.pallas_call(
        _outproj_body,
        out_shape=(jax.ShapeDtypeStruct((t, d), F32), jax.ShapeDtypeStruct((t, d // LANE, LANE), F32),
                   jax.ShapeDtypeStruct((t, LANE), I32), jax.ShapeDtypeStruct((t, LANE), F32)),
        grid=(t // tm,),
        in_specs=[row(HEAD_COLS), row(HEAD_COLS), row(HEAD_COLS), row(HEAD_COLS), row(d),
                  const(woa), const(wob), const(wout), const(g), const(b), const(wr), const(br)],
        out_specs=(row(d), pl.BlockSpec((tm, d // LANE, LANE), lambda i: (i, 0, 0)), row(LANE), row(LANE)),
        compiler_params=_params(("parallel",)),
        name="outproj",
    )(oa, ob, sga, sgb, x, woa, wob, wout, g, b, wr, br)


def _store_row_tiles(ref, x):
    for j in range(x.shape[1] // LANE):
        ref[:, j, :] = x[:, j * LANE:(j + 1) * LANE]


def _load_row_tiles(ref):
    return jnp.concatenate([ref[:, j, :] for j in range(ref.shape[1])], axis=1)


def _rank_body(idx_ref, rank_ref, cnt_ref, run_sc):
    tr = idx_ref.shape[0]

    @pl.when(pl.program_id(0) == 0)
    def _():
        run_sc[...] = jnp.zeros(run_sc.shape, F32)

    idx = idx_ref[...]
    lane = lax.broadcasted_iota(I32, (tr, LANE), 1)
    chosen = [idx[:, k:k + 1] == lane for k in range(TOP_K)]
    member = functools.reduce(lambda a, b: a | b, chosen).astype(F32)
    below = (lax.broadcasted_iota(I32, (tr, tr), 0) > lax.broadcasted_iota(I32, (tr, tr), 1)).astype(BF16)
    before = _dot(below, member.astype(BF16)) + run_sc[...]
    rank = jnp.zeros((tr, LANE), F32)
    for k in range(TOP_K):
        rank = jnp.where(lane == k, jnp.sum(jnp.where(chosen[k], before, 0.0), axis=-1, keepdims=True), rank)
    rank_ref[...] = rank.astype(I32)
    run_sc[...] = run_sc[...] + jnp.sum(member, axis=0, keepdims=True)
    cnt_ref[...] = run_sc[...]


def _rank_call(top_idx):
    t = top_idx.shape[0]
    tr = _pick_tile(t, 256)
    row = pl.BlockSpec((tr, LANE), lambda i: (i, 0))
    return pl.pallas_call(
        _rank_body,
        out_shape=(jax.ShapeDtypeStruct((t, LANE), I32), jax.ShapeDtypeStruct((1, LANE), F32)),
        grid=(t // tr,),
        in_specs=[row],
        out_specs=(row, pl.BlockSpec((1, LANE), lambda i: (0, 0))),
        scratch_shapes=[pltpu.VMEM((1, LANE), F32)],
        compiler_params=_params(("arbitrary",)),
        name="moe_rank",
    )(top_idx)


def _route(top_idx, bm):
    t = top_idx.shape[0]
    rank, counts = _rank_call(top_idx)
    counts = counts[0, :N_EXPERTS].astype(I32)
    padded = (counts + bm - 1) // bm * bm
    pad_end = jnp.cumsum(padded)
    pad_start = pad_end - padded
    n_blocks = -(-t * TOP_K // bm) + N_EXPERTS
    block_e = jnp.minimum(
        jnp.sum(pad_end[None, :] <= (jnp.arange(n_blocks, dtype=I32) * bm)[:, None], axis=1), N_EXPERTS - 1).astype(I32)
    n_used = (pad_end[-1:] // bm).astype(I32)
    idx = top_idx[:, :TOP_K]
    start_of = jnp.sum(jnp.where(idx[:, :, None] == jnp.arange(N_EXPERTS, dtype=I32), pad_start, 0), axis=-1)
    return block_e, n_used, (start_of + rank[:, :TOP_K]).astype(I32)


DMA_UNROLL = 16


def _for_row_groups(n, fn):
    assert n % DMA_UNROLL == 0

    def body(g, c):
        for u in range(DMA_UNROLL):
            fn(g * DMA_UNROLL + u, u)
        return c

    lax.fori_loop(0, n // DMA_UNROLL, body, 0)


def _dispatch_body(dest_ref, x_ref, xs_in_hbm, xs_hbm, sem, *, td):
    del xs_in_hbm
    rows = TOP_K * td
    assert DMA_UNROLL % TOP_K == 0

    def start(r, u):
        tok = (r - u) // TOP_K + u // TOP_K
        pltpu.make_async_copy(x_ref.at[tok], xs_hbm.at[dest_ref[0, 0, r]], sem).start(priority=u % 2)

    def wait(r, u):
        pltpu.make_async_copy(x_ref.at[0], xs_hbm.at[0], sem).wait()

    _for_row_groups(rows, start)
    _for_row_groups(rows, wait)


def _dispatch_call(dest, x_tiles, xs_buf, *, td):
    t = x_tiles.shape[0]
    n = t // td
    rows = TOP_K * td
    return pl.pallas_call(
        functools.partial(_dispatch_body, td=td),
        out_shape=jax.ShapeDtypeStruct(xs_buf.shape, F32),
        grid=(n,),
        in_specs=[pl.BlockSpec((1, 1, rows), lambda i: (i, 0, 0), memory_space=pltpu.SMEM),
                  pl.BlockSpec((td,) + x_tiles.shape[1:], lambda i: (i, 0, 0)), pl.BlockSpec(memory_space=pl.ANY)],
        out_specs=pl.BlockSpec(memory_space=pl.ANY),
        scratch_shapes=[pltpu.SemaphoreType.DMA],
        input_output_aliases={2: 0},
        compiler_params=_params(("arbitrary",)),
        name="moe_dispatch",
    )(dest.reshape(n, 1, rows), x_tiles, xs_buf)


def _moe_body(be_ref, nb_ref, xs_ref, wgu_ref, bgu_ref, wdn_ref, bdn_ref, y_ref):
    @pl.when(pl.program_id(0) < nb_ref[0])
    def _():
        x = _load_row_tiles(xs_ref).astype(BF16)
        h = _dot(x, wgu_ref[0]) + bgu_ref[0]
        g = jnp.minimum(h[:, :D_FF], SWIGLU_LIMIT)
        u = jnp.clip(h[:, D_FF:], -SWIGLU_LIMIT, SWIGLU_LIMIT)
        act = (u + 1.0) * (g * jax.nn.sigmoid(SWIGLU_ALPHA * g))
        _store_row_tiles(y_ref, _dot(act.astype(BF16), wdn_ref[0]) + bdn_ref[0])

    @pl.when(pl.program_id(0) >= nb_ref[0])
    def _():
        y_ref[...] = jnp.zeros(y_ref.shape, F32)


def _moe_call(block_e, n_used, xs, wgu, bgu, wdn, bdn, *, bm):
    n_blocks = block_e.shape[0]
    d = D_MODEL
    tiles = pl.BlockSpec((bm, d // LANE, LANE), lambda b, be, nb: (b, 0, 0))
    grid_spec = pltpu.PrefetchScalarGridSpec(
        num_scalar_prefetch=2,
        grid=(n_blocks,),
        in_specs=[
            tiles,
            pl.BlockSpec((1, d, 2 * D_FF), lambda b, be, nb: (be[b], 0, 0)),
            pl.BlockSpec((1, 1, 2 * D_FF), lambda b, be, nb: (be[b], 0, 0)),
            pl.BlockSpec((1, D_FF, d), lambda b, be, nb: (be[b], 0, 0)),
            pl.BlockSpec((1, 1, d), lambda b, be, nb: (be[b], 0, 0)),
        ],
        out_specs=tiles,
    )
    return pl.pallas_call(
        _moe_body,
        out_shape=jax.ShapeDtypeStruct(xs.shape, F32),
        grid_spec=grid_spec,
        compiler_params=_params(("arbitrary",)),
        name="moe_experts",
    )(block_e, n_used, xs, wgu, bgu.reshape(N_EXPERTS, 1, 2 * D_FF), wdn, bdn.reshape(N_EXPERTS, 1, d))


def _start_row_gather(idx_ref, n, src_hbm, dst, sem):
    _for_row_groups(n, lambda r, u: pltpu.make_async_copy(
        src_hbm.at[idx_ref[0, 0, r]], dst.at[r], sem).start(priority=u % 2))


def _wait_row_gather(n, src_hbm, dst, sem):
    _for_row_groups(n, lambda r, u: pltpu.make_async_copy(src_hbm.at[0], dst.at[r], sem).wait())


def _combine_body(dest_ref, destn_ref, y_hbm, x_ref, gate_ref, g_ref, b_ref, o_ref, ybuf, sem, *, tc):
    i = pl.program_id(0)
    n = pl.num_programs(0)
    slot = i % 2
    rows = TOP_K * tc

    @pl.when(i == 0)
    def _():
        _start_row_gather(dest_ref, rows, y_hbm, ybuf.at[0], sem.at[0])

    @pl.when(i + 1 < n)
    def _():
        _start_row_gather(destn_ref, rows, y_hbm, ybuf.at[1 - slot], sem.at[1 - slot])

    _wait_row_gather(rows, y_hbm, ybuf.at[slot], sem.at[slot])
    gate = gate_ref[...]
    moe = jnp.zeros(x_ref.shape, F32)
    for k in range(TOP_K):
        moe = moe + gate[:, k:k + 1] * _load_row_tiles(ybuf.at[slot, k * tc:(k + 1) * tc])
    o_ref[...] = _layer_norm(ALPHA * x_ref[...] + moe, g_ref[...], b_ref[...])


def _combine_call(dest, y, x, gate, g, b, *, tc):
    t, d = x.shape
    n = t // tc
    rows = TOP_K * tc
    dest3 = dest.reshape(n, tc, TOP_K).transpose(0, 2, 1).reshape(n, 1, rows)
    smem_row = lambda f: pl.BlockSpec((1, 1, rows), f, memory_space=pltpu.SMEM)
    row = lambda w: pl.BlockSpec((tc, w), lambda i: (i, 0))
    const = lambda a: pl.BlockSpec(a.shape, lambda i: (0, 0))
    return pl.pallas_call(
        functools.partial(_combine_body, tc=tc),
        out_shape=jax.ShapeDtypeStruct((t, d), F32),
        grid=(n,),
        in_specs=[smem_row(lambda i: (i, 0, 0)), smem_row(lambda i: (jnp.minimum(i + 1, n - 1), 0, 0)),
                  pl.BlockSpec(memory_space=pl.ANY), row(d), row(LANE), const(g), const(b)],
        out_specs=row(d),
        scratch_shapes=[pltpu.VMEM((2, rows, d // LANE, LANE), F32), pltpu.SemaphoreType.DMA((2,))],
        compiler_params=_params(("arbitrary",)),
        name="moe_combine",
    )(dest3, dest3, y, x, gate, g, b)


def _head_blocks(w, widths):
    per = sum(widths)
    w = w.reshape(*w.shape[:-1], H_A, per)
    pad = [(0, 0)] * (w.ndim - 1) + [(0, LANE - per)]
    return jnp.pad(w, pad).reshape(*w.shape[:-2], HEAD_COLS)


def _rotate_half_cols(w):
    half = ROPE_DIM // 2
    return jnp.concatenate([-w[..., half:], w[..., :half]], axis=-1)


def _prep_layer(w_in, w_uq, w_ukv, w_oa):
    d = w_in.shape[0]
    splits = (Q_LORA, KV_LORA, ROPE_DIM, 1024, 1024, 1024, D_MODEL, D_MODEL)
    offs = [0]
    for s in splits:
        offs.append(offs[-1] + s)
    lat = w_in[:, offs[0]:offs[2]]
    kr = w_in[:, offs[2]:offs[3]]
    rest = w_in[:, offs[3]:]

    def rope_block(w):
        return jnp.pad(w, ((0, 0), (NOPE_DIM, LANE - NOPE_DIM - ROPE_DIM)))

    w1 = jnp.concatenate([lat, rope_block(kr), rope_block(_rotate_half_cols(kr)), rest], axis=1).astype(BF16)
    assert w1.shape == (d, C_END)
    wvt = w_in[:, offs[5]:offs[6]].T.astype(BF16)

    uq = w_uq.reshape(Q_LORA, H_A, NOPE_DIM + ROPE_DIM)
    uq_rot = jnp.concatenate([jnp.zeros((Q_LORA, H_A, NOPE_DIM), F32), _rotate_half_cols(uq[..., NOPE_DIM:])], -1)
    wuq = jnp.concatenate([_head_blocks(uq.reshape(Q_LORA, -1), (NOPE_DIM + ROPE_DIM,)),
                           _head_blocks(uq_rot.reshape(Q_LORA, -1), (NOPE_DIM + ROPE_DIM,))], axis=1).astype(BF16)

    ukv = w_ukv.reshape(KV_LORA, H_A, NOPE_DIM + V_DIM_A)
    wuk = _head_blocks(ukv[..., :NOPE_DIM].reshape(KV_LORA, -1), (NOPE_DIM,)).astype(BF16)
    wuv = _head_blocks(ukv[..., NOPE_DIM:].reshape(KV_LORA, -1), (V_DIM_A,)).astype(BF16)
    wuvt = ukv[..., NOPE_DIM:].reshape(KV_LORA, H_A * V_DIM_A).T.astype(BF16)

    woa = jnp.pad(w_oa.reshape(H_A, V_DIM_A, D_MODEL), ((0, 0), (0, LANE - V_DIM_A), (0, 0)))
    woa = woa.reshape(HEAD_COLS, D_MODEL).astype(BF16)
    return w1, wvt, wuq, wuk, wuv, wuvt, woa


def _rope_tables(pos):
    half = ROPE_DIM // 2
    inv = ROPE_THETA ** (-jnp.arange(half, dtype=F32) / half)
    ang = pos.astype(F32)[:, None] * inv[None, :]
    cos, sin = jnp.cos(ang), jnp.sin(ang)
    n = pos.shape[0]
    tail = jnp.zeros((n, LANE - NOPE_DIM - ROPE_DIM), F32)
    cos_t = jnp.concatenate([jnp.ones((n, NOPE_DIM), F32), cos, cos, tail], axis=1)
    sin_t = jnp.concatenate([jnp.zeros((n, NOPE_DIM), F32), sin, sin, tail], axis=1)
    return cos_t, sin_t


def _rope_block_of(kr):
    return jnp.pad(kr, ((0, 0), (NOPE_DIM, LANE - NOPE_DIM - ROPE_DIM)))


def _moe_block_rows(t):
    return 256 if t * TOP_K >= 256 * N_EXPERTS else 128


def _moe_row_buffer(t):
    bm = _moe_block_rows(t)
    return jnp.zeros(((-(-t * TOP_K // bm) + N_EXPERTS) * bm, D_MODEL // LANE, LANE), F32)


def _moe_layer(x1, x1_tiles, top_idx, gate, rows_buf, wgu, bgu, wdn, bdn, g, b):
    t = x1.shape[0]
    bm = _moe_block_rows(t)
    tc = _pick_tile(t, 128)
    block_e, n_used, dest = _route(top_idx, bm)
    xs = _dispatch_call(dest, x1_tiles, rows_buf, td=_pick_tile(t, 512))
    y = _moe_call(block_e, n_used, xs, wgu, bgu, wdn, bdn, bm=bm)
    return _combine_call(dest, y, x1, gate, g, b, tc=tc), y


def kernel(x_prompt, x_sample, cache_mla_latent, cache_mla_krope, cache_diff_k, cache_diff_v,
           ln_in_g, ln_in_b, w_in, q_norm_g, kv_norm_g, w_uq, w_ukv, lam_q1, lam_k1, lam_q2, lam_k2,
           subln_g, w_oa, w_ob, w_out, ln1_g, ln1_b, w_router, b_router, w_gu, b_gu, w_dn, b_dn,
           ln2_g, ln2_b):
    bp, sp, d = x_prompt.shape
    bs, ss, _ = x_sample.shape
    past = cache_mla_latent.shape[2]
    assert bp == 1 and d == D_MODEL
    tp, ts = bp * sp, bs * ss

    cos_p, sin_p = _rope_tables(jnp.arange(sp, dtype=I32))
    cos_s, sin_s = _rope_tables(jnp.tile(past + jnp.arange(ss, dtype=I32), bs))
    slopes = 2.0 ** (-8.0 * (jnp.arange(H_B, dtype=F32) + 1.0) / H_B)
    slopes = jnp.broadcast_to(slopes[:, None, None], (H_B, 1, LANE))
    tk = _pick_tile(sp, 256)
    tq_mla = _pick_tile(sp, 512)
    tq_diff = _pick_tile(sp, 256)

    xp = _ln_call(x_prompt.reshape(tp, d), ln_in_g, ln_in_b)
    xs = _ln_call(x_sample.reshape(ts, d), ln_in_g, ln_in_b)
    rows_p, rows_s = _moe_row_buffer(tp), _moe_row_buffer(ts)
    new_p = ([], [], [], [])
    new_s = ([], [], [], [])
    for l in range(DEPTH):
        lam_init = 0.8 - 0.6 * math.exp(-0.3 * l)
        w1, wvt, wuq, wuk, wuv, wuvt, woa = _prep_layer(w_in[l], w_uq[l], w_ukv[l], w_oa[l])
        gq = q_norm_g[l].reshape(1, Q_LORA)
        gkv = kv_norm_g[l].reshape(1, KV_LORA)
        lamv = jnp.stack([lam_q1[l], lam_k1[l], lam_q2[l], lam_k2[l]])
        subg = subln_g[l].reshape(1, 2 * DH_B)
        wob = w_ob[l].astype(BF16)
        wout = w_out[l].astype(BF16)
        g1, b1 = ln1_g[l].reshape(1, d), ln1_b[l].reshape(1, d)
        g2, b2 = ln2_g[l].reshape(1, d), ln2_b[l].reshape(1, d)
        wr_hi = w_router[l].astype(BF16)
        wr = jnp.concatenate([wr_hi, (w_router[l] - wr_hi.astype(F32)).astype(BF16)], axis=1)
        br = b_router[l].reshape(1, N_EXPERTS)
        wgu, wdn = w_gu[l].astype(BF16), w_dn[l].astype(BF16)
        diff_extra = [slopes, lamv, subg]

        def diff_specs(head_of):
            return [pl.BlockSpec((1, 1, LANE), lambda a, b: (head_of(a, b), 0, 0)),
                    pl.BlockSpec(lamv.shape, lambda a, b: (0, 0)), pl.BlockSpec(subg.shape, lambda a, b: (0, 0))]

        qa, lat, krb, qd, kd32, kd16, vd32, _, vdt, sga, sgb = _inproj_call(xp, w1, wvt, wuq, gq, gkv, cos_p, sin_p)
        ka, vat = _kvup_call(lat, krb, wuk, wuvt, feature_major_v=True)
        oa = _prompt_attn_call(_mla_prompt_body, qa, ka, vat, [], [], maps=1, tq=tq_mla, tk=tk, unroll=MLA_UNROLL,
                               name="mla_prompt")
        ob = _prompt_attn_call(functools.partial(_diff_prompt_body, lam_init=lam_init), qd, kd16, vdt,
                               diff_extra, diff_specs(lambda h, i: h), maps=2, tq=tq_diff, tk=tk, unroll=DIFF_UNROLL,
                               name="diff_prompt")
        xp1, xp1t, idx_p, gate_p = _outproj_call(oa, ob, sga, sgb, xp, woa, wob, wout, g1, b1, wr, br)
        new_p[0].append(lat.reshape(bp, sp, KV_LORA))
        new_p[1].append(krb[:, NOPE_DIM:NOPE_DIM + ROPE_DIM].reshape(bp, sp, ROPE_DIM))
        new_p[2].append(kd32.reshape(bp, sp, 2 * H_B, DH_B))
        new_p[3].append(vd32.reshape(bp, sp, H_B, 2 * DH_B))

        qa, lat, krb, qd, kd32, kd16, vd32, vd16, _, sga, sgb = _inproj_call(xs, w1, wvt, wuq, gq, gkv, cos_s, sin_s)
        kn, vn = _kvup_call(lat, krb, wuk, wuv, feature_major_v=False)
        kpast, vpast = _kvup_call(cache_mla_latent[l].reshape(bs * past, KV_LORA),
                                  _rope_block_of(cache_mla_krope[l].reshape(bs * past, ROPE_DIM)), wuk, wuv,
                                  feature_major_v=False)
        oa = _sample_attn_call(_mla_sample_body, qa, kpast, vpast, kn, vn, [], [], batch=bs, past=past,
                               name="mla_sample")
        ob = _sample_attn_call(functools.partial(_diff_sample_body, lam_init=lam_init), qd,
                               cache_diff_k[l].reshape(bs * past, HEAD_COLS),
                               cache_diff_v[l].reshape(bs * past, HEAD_COLS), kd16, vd16,
                               diff_extra, diff_specs(lambda b, h: h), batch=bs, past=past, name="diff_sample")
        xs1, xs1t, idx_s, gate_s = _outproj_call(oa, ob, sga, sgb, xs, woa, wob, wout, g1, b1, wr, br)
        new_s[0].append(lat.reshape(bs, ss, KV_LORA))
        new_s[1].append(krb[:, NOPE_DIM:NOPE_DIM + ROPE_DIM].reshape(bs, ss, ROPE_DIM))
        new_s[2].append(kd32.reshape(bs, ss, 2 * H_B, DH_B))
        new_s[3].append(vd32.reshape(bs, ss, H_B, 2 * DH_B))

        xp, rows_p = _moe_layer(xp1, xp1t, idx_p, gate_p, rows_p, wgu, b_gu[l], wdn, b_dn[l], g2, b2)
        xs, rows_s = _moe_layer(xs1, xs1t, idx_s, gate_s, rows_s, wgu, b_gu[l], wdn, b_dn[l], g2, b2)

    return (xp.reshape(bp, sp, d), xs.reshape(bs, ss, d),
            jnp.stack(new_p[0]), jnp.stack(new_p[1]), jnp.stack(new_p[2]), jnp.stack(new_p[3]),
            jnp.stack(new_s[0]), jnp.stack(new_s[1]), jnp.stack(new_s[2]), jnp.stack(new_s[3]))
```

```python
import functools
import math

import jax
import jax.numpy as jnp
from jax import lax
from jax.experimental import pallas as pl
from jax.experimental.pallas import tpu as pltpu

F32 = jnp.float32
BF16 = jnp.bfloat16
I32 = jnp.int32

D_MODEL = 1024
DEPTH = 4
CHUNK = 64
CHUNK_SHIFT = 6
H_A = 8
NOPE_DIM = 64
ROPE_DIM = 32
V_DIM_A = 64
Q_LORA = 256
KV_LORA = 256
ROPE_THETA = 10000.0
H_B = 8
DH_B = 64
N_EXPERTS = 32
TOP_K = 4
D_FF = 1024
SWIGLU_LIMIT = 7.0
SWIGLU_ALPHA = 1.702
LN_EPS = 1e-5
RMS_EPS = 1e-6
ALPHA = (2 * DEPTH) ** 0.25

LANE = 128
BF16_ROWS = 16
HEAD_COLS = H_A * LANE
LOG2E = 1.4426950408889634
MLA_QSCALE = (NOPE_DIM + ROPE_DIM) ** -0.5 * LOG2E
DIFF_QSCALE = DH_B ** -0.5 * LOG2E
MLA_UNROLL = 8
DIFF_UNROLL = 8
NEG = -1e30
VMEM_LIMIT = 48 * 1024 * 1024

C_LAT = 0
C_KR = 512
C_KRROT = 640
C_QD = 768
C_KD = C_QD + 1024
C_VD = C_KD + 1024
C_GA = C_VD + 1024
C_GB = C_GA + 1024
C_END = C_GB + 1024


def _pick_tile(n, pref):
    t = min(n, pref)
    while n % t:
        t -= 8
    assert t > 0 and (t % 8 == 0 or t == n), (n, pref)
    return t


def _params(sem):
    return pltpu.CompilerParams(dimension_semantics=sem, vmem_limit_bytes=VMEM_LIMIT)


def _layer_norm(x, g, b):
    mu = jnp.mean(x, axis=-1, keepdims=True)
    xc = x - mu
    var = jnp.mean(xc * xc, axis=-1, keepdims=True)
    return xc * lax.rsqrt(var + LN_EPS) * g + b


def _rms_norm(x, g, eps):
    ms = jnp.mean(x * x, axis=-1, keepdims=True)
    return x * lax.rsqrt(ms + eps) * g


def _dot(a, b):
    return jnp.dot(a, b, preferred_element_type=F32)


def _dot_nt(a, b):
    return lax.dot_general(a, b, (((1,), (1,)), ((), ())), preferred_element_type=F32)


def _ln_body(x_ref, g_ref, b_ref, o_ref):
    o_ref[...] = _layer_norm(x_ref[...], g_ref[...], b_ref[...])


def _ln_call(x, g, b):
    t, d = x.shape
    tm = _pick_tile(t, 512)
    row = pl.BlockSpec((tm, d), lambda i: (i, 0))
    vec = pl.BlockSpec((1, d), lambda i: (0, 0))
    return pl.pallas_call(
        _ln_body,
        out_shape=jax.ShapeDtypeStruct((t, d), F32),
        grid=(t // tm,),
        in_specs=[row, vec, vec],
        out_specs=row,
        compiler_params=_params(("parallel",)),
        name="ln_in",
    )(x, g.reshape(1, d), b.reshape(1, d))


def _inproj_body(x_ref, w_ref, wvt_ref, wuq_ref, gq_ref, gkv_ref, cos_ref, sin_ref,
                 q_ref, lat_ref, krb_ref, qd_ref, kd32_ref, kd16_ref, vd32_ref, vd16_ref, vdt_ref,
                 sga_ref, sgb_ref):
    xb = x_ref[...].astype(BF16)
    cos = cos_ref[...]
    sin = sin_ref[...]

    def proj(lo, hi):
        return _dot(xb, w_ref[:, lo:hi])

    z = proj(C_LAT, C_QD)
    cq = _rms_norm(z[:, :Q_LORA], gq_ref[...], RMS_EPS)
    lat_ref[...] = _rms_norm(z[:, Q_LORA:C_KR], gkv_ref[...], RMS_EPS)
    krb_ref[...] = z[:, C_KR:C_KRROT] * cos + z[:, C_KRROT:C_QD] * sin
    qq = _dot(cq.astype(BF16), wuq_ref[...])
    for h in range(H_A):
        lo = h * LANE
        qh = qq[:, lo:lo + LANE] * cos + qq[:, HEAD_COLS + lo:HEAD_COLS + lo + LANE] * sin
        q_ref[:, lo:lo + LANE] = (qh * MLA_QSCALE).astype(BF16)
    qd_ref[...] = (proj(C_QD, C_KD) * DIFF_QSCALE).astype(BF16)
    z = proj(C_KD, C_VD)
    kd32_ref[...] = z
    kd16_ref[...] = z.astype(BF16)
    z = proj(C_VD, C_GA)
    vd32_ref[...] = z
    vd16_ref[...] = z.astype(BF16)
    vdt_ref[...] = _dot_nt(wvt_ref[...], xb).astype(BF16)
    sga_ref[...] = jax.nn.sigmoid(proj(C_GA, C_GB)).astype(BF16)
    sgb_ref[...] = jax.nn.sigmoid(proj(C_GB, C_END)).astype(BF16)


def _inproj_call(x, w, wvt, wuq, gq, gkv, cos_t, sin_t):
    t, d = x.shape
    tm = _pick_tile(t, 256)
    row = lambda n: pl.BlockSpec((tm, n), lambda i: (i, 0))
    const = lambda a: pl.BlockSpec(a.shape, lambda i: (0, 0), pipeline_mode=pl.Buffered(1))
    wide = lambda dt: jax.ShapeDtypeStruct((t, HEAD_COLS), dt)
    return pl.pallas_call(
        _inproj_body,
        out_shape=(wide(BF16), jax.ShapeDtypeStruct((t, KV_LORA), F32), jax.ShapeDtypeStruct((t, LANE), F32),
                   wide(BF16), wide(F32), wide(BF16), wide(F32), wide(BF16),
                   jax.ShapeDtypeStruct((HEAD_COLS, t), BF16), wide(BF16), wide(BF16)),
        grid=(t // tm,),
        in_specs=[row(d), const(w), const(wvt), const(wuq), const(gq), const(gkv), row(LANE), row(LANE)],
        out_specs=(row(HEAD_COLS), row(KV_LORA), row(LANE), row(HEAD_COLS), row(HEAD_COLS), row(HEAD_COLS),
                   row(HEAD_COLS), row(HEAD_COLS), pl.BlockSpec((HEAD_COLS, tm), lambda i: (0, i)),
                   row(HEAD_COLS), row(HEAD_COLS)),
        compiler_params=_params(("parallel",)),
        name="inproj",
    )(x, w, wvt, wuq, gq, gkv, cos_t, sin_t)


def _kvup_body(lat_ref, krb_ref, wk_ref, wv_ref, k_ref, v_ref, *, feature_major_v):
    lat = lat_ref[...].astype(BF16)
    kn = _dot(lat, wk_ref[...])
    krb = krb_ref[...]
    for h in range(H_A):
        lo = h * LANE
        k_ref[:, lo:lo + LANE] = (kn[:, lo:lo + LANE] + krb).astype(BF16)
    if feature_major_v:
        v_ref[...] = _dot_nt(wv_ref[...], lat).astype(BF16)
    else:
        v_ref[...] = _dot(lat, wv_ref[...]).astype(BF16)


def _kvup_call(lat, krb, wk, wv, *, feature_major_v):
    t = lat.shape[0]
    tm = _pick_tile(t, 512)
    row = lambda n: pl.BlockSpec((tm, n), lambda i: (i, 0))
    const = lambda a: pl.BlockSpec(a.shape, lambda i: (0, 0))
    if feature_major_v:
        v_shape, v_spec = (H_A * V_DIM_A, t), pl.BlockSpec((H_A * V_DIM_A, tm), lambda i: (0, i))
    else:
        v_shape, v_spec = (t, HEAD_COLS), row(HEAD_COLS)
    return pl.pallas_call(
        functools.partial(_kvup_body, feature_major_v=feature_major_v),
        out_shape=(jax.ShapeDtypeStruct((t, HEAD_COLS), BF16), jax.ShapeDtypeStruct(v_shape, BF16)),
        grid=(t // tm,),
        in_specs=[row(KV_LORA), row(LANE), const(wk), const(wv)],
        out_specs=(row(HEAD_COLS), v_spec),
        compiler_params=_params(("parallel",)),
        name="kvup",
    )(lat, krb, wk, wv)


def _stack_diff_queries(q):
    lane = lax.broadcasted_iota(I32, q.shape, 1)
    zero = jnp.zeros_like(q)
    return jnp.concatenate([jnp.where(lane < DH_B, q, zero), jnp.where(lane >= DH_B, q, zero)], axis=0)


def _diff_lambda(lamv_ref, lam_init):
    lv = lamv_ref[...]
    a = jnp.sum(lv[0:1] * lv[1:2], axis=-1, keepdims=True)
    b = jnp.sum(lv[2:3] * lv[3:4], axis=-1, keepdims=True)
    return jnp.exp(a) - jnp.exp(b) + lam_init


def _ones_row_block(tk):
    return (lax.broadcasted_iota(I32, (BF16_ROWS, tk), 0) == 0).astype(BF16)


def _visible_strip(q0, k0, tk):
    c = k0 + lax.broadcasted_iota(I32, (tk, LANE), 0)
    r = q0 + lax.broadcasted_iota(I32, (tk, LANE), 1)
    return (c >> CHUNK_SHIFT) <= (r >> CHUNK_SHIFT)


def _weights_and_rescale(s_ref, p_ref, m_prev, mx, offset):
    m_new = jnp.maximum(m_prev, mx + offset)
    p_ref[...] = jnp.exp2(s_ref[...] - (m_new - offset)).astype(BF16)
    return m_new, jnp.exp2(m_prev - m_new)


def _sweep_key_tiles(scores, values, offset, s_sc, p_sc, acc_sc, q0, tq, tk, unroll):
    width = s_sc.shape[2]
    acc_sc[...] = jnp.zeros(acc_sc.shape, F32)
    p_sc[1] = jnp.zeros(p_sc.shape[1:], BF16)
    n_full = q0 // tk
    s0 = scores(0, False)
    s_sc[0] = s0

    def step(j, cur, carry):
        m_prev, alpha_prev, mx = carry
        s_next = scores(j + 1, False)
        s_sc[1 - cur] = s_next
        acc_sc[...] = alpha_prev * acc_sc[...] + _dot(values(jnp.maximum(j - 1, 0)), p_sc[1 - cur])
        m_new, alpha = _weights_and_rescale(s_sc.at[cur], p_sc.at[cur], m_prev, mx, offset(j))
        return m_new, alpha, jnp.max(s_next, axis=0, keepdims=True)

    def steps(n, base, carry):
        for u in range(n):
            carry = step(base + u, u % 2, carry)
        return carry

    carry = (jnp.full((1, width), NEG, F32), jnp.ones((1, width), F32), jnp.max(s0, axis=0, keepdims=True))
    carry = lax.fori_loop(0, n_full // unroll, lambda jj, c: steps(unroll, unroll * jj, c), carry)
    n = unroll // 2
    while n:
        base = n_full // (2 * n) * (2 * n)
        carry = lax.fori_loop(0, (n_full // n) % 2, lambda _, c, n=n, base=base: steps(n, base, c), carry)
        n //= 2
    m, alpha, _ = carry
    last = (n_full + 1) % 2
    masked = [scores(n_full + j, True) for j in range(max(1, tq // tk))]
    acc_sc[...] = alpha * acc_sc[...] + _dot(values(jnp.maximum(n_full - 1, 0)), p_sc[last])
    for j, s in enumerate(masked):
        m_new = jnp.maximum(m, jnp.max(s, axis=0, keepdims=True))
        p = jnp.exp2(s - m_new).astype(BF16)
        acc_sc[...] = jnp.exp2(m - m_new) * acc_sc[...] + _dot(values(n_full + j), p)
        m = m_new
    return acc_sc[...]


def _mla_prompt_body(q_ref, k_ref, vt_ref, o_ref, s_sc, p_sc, acc_sc, *, tq, tk, unroll):
    q0 = pl.program_id(1) * tq
    q = q_ref[...]
    ones = _ones_row_block(tk)

    def key_start(kt):
        return pl.multiple_of(kt * tk, tk)

    def scores(kt, masked):
        s = _dot_nt(k_ref[pl.ds(key_start(kt), tk), :], q)
        if masked:
            vis = jnp.concatenate([_visible_strip(q0 + c * LANE, kt * tk, tk) for c in range(tq // LANE)], axis=1)
            s = jnp.where(vis, s, NEG)
        return s

    def values(kt):
        return jnp.concatenate([vt_ref[:, pl.ds(key_start(kt), tk)], ones], axis=0)

    acc = _sweep_key_tiles(scores, values, lambda kt: 0.0, s_sc, p_sc, acc_sc, q0, tq, tk, unroll)
    o = acc[:V_DIM_A] / acc[V_DIM_A:V_DIM_A + 1]
    o_ref[:, :V_DIM_A] = o.T.astype(o_ref.dtype)
    o_ref[:, V_DIM_A:] = jnp.zeros((tq, LANE - V_DIM_A), o_ref.dtype)


def _diff_prompt_body(q_ref, k_ref, vt_ref, slope_ref, lamv_ref, g_ref, o_ref, s_sc, p_sc, acc_sc,
                      *, tq, tk, unroll, lam_init):
    q0 = pl.program_id(1) * tq
    qs = _stack_diff_queries(q_ref[...])
    slope = slope_ref[0][:, :1] * LOG2E
    ones = _ones_row_block(tk)

    def key_start(kt):
        return pl.multiple_of(kt * tk, tk)

    row_bias = slope * lax.broadcasted_iota(I32, (tk, LANE), 0).astype(F32)

    def offset(kt):
        return slope * (kt * tk - q0).astype(F32)

    def scores(kt, masked):
        s = _dot_nt(k_ref[pl.ds(key_start(kt), tk), :], qs)
        if not masked:
            return s + jnp.concatenate([row_bias] * (2 * tq // LANE), axis=1)
        c_rel = (kt * tk - q0 + lax.broadcasted_iota(I32, (tk, LANE), 0)).astype(F32)
        strips = []
        for c in range(tq // LANE):
            a = (c * LANE + lax.broadcasted_iota(I32, (tk, LANE), 1)).astype(F32)
            bias = slope * jnp.minimum(c_rel, 2.0 * a - c_rel)
            strips.append(jnp.where(_visible_strip(q0 + c * LANE, kt * tk, tk), bias, NEG))
        return s + jnp.concatenate(strips + strips, axis=1)

    def values(kt):
        return jnp.concatenate([vt_ref[:, pl.ds(key_start(kt), tk)], ones], axis=0)

    acc = _sweep_key_tiles(scores, values, offset, s_sc, p_sc, acc_sc, q0, tq, tk, unroll)
    dv = 2 * DH_B
    o1 = acc[:dv, :tq] / acc[dv:dv + 1, :tq]
    o2 = acc[:dv, tq:] / acc[dv:dv + 1, tq:]
    o = o1 - _diff_lambda(lamv_ref, lam_init) * o2
    o = o * lax.rsqrt(jnp.mean(o * o, axis=0, keepdims=True) + RMS_EPS)
    o_ref[...] = (o.T * g_ref[...] * (1.0 - lam_init)).astype(o_ref.dtype)


def _prompt_attn_call(body, q, k, vt, extra, extra_specs, *, maps, tq, tk, unroll, name):
    t = q.shape[0]
    dv = vt.shape[0] // H_A
    width = maps * tq
    assert t % tq == 0 and t % tk == 0 and (tq % tk == 0 or tk % tq == 0) and tq % LANE == 0 and tk % CHUNK == 0
    qspec = pl.BlockSpec((tq, LANE), lambda h, i: (i, h))
    return pl.pallas_call(
        functools.partial(body, tq=tq, tk=tk, unroll=unroll),
        out_shape=jax.ShapeDtypeStruct((t, HEAD_COLS), BF16),
        grid=(H_A, t // tq),
        in_specs=[qspec, pl.BlockSpec((t, LANE), lambda h, i: (0, h)),
                  pl.BlockSpec((dv, t), lambda h, i: (h, 0))] + extra_specs,
        out_specs=qspec,
        scratch_shapes=[pltpu.VMEM((2, tk, width), F32), pltpu.VMEM((2, tk, width), BF16),
                        pltpu.VMEM((dv + BF16_ROWS, width), F32)],
        compiler_params=_params(("parallel", "arbitrary")),
        name=name,
    )(q, k, vt, *extra)


def _chunk_visible(q0, k0, shape):
    r = q0 + lax.broadcasted_iota(I32, shape, 0)
    c = k0 + lax.broadcasted_iota(I32, shape, 1)
    return (c >> CHUNK_SHIFT) <= (r >> CHUNK_SHIFT)


def _two_part_softmax_pv(s_past, s_new, v_past, v_new):
    m = jnp.maximum(jnp.max(s_past, axis=-1, keepdims=True), jnp.max(s_new, axis=-1, keepdims=True))
    p_past = jnp.exp2(s_past - m)
    p_new = jnp.exp2(s_new - m)
    l = jnp.sum(p_past, axis=-1, keepdims=True) + jnp.sum(p_new, axis=-1, keepdims=True)
    acc = _dot(p_past.astype(BF16), v_past) + _dot(p_new.astype(BF16), v_new)
    return acc / l


def _mla_sample_body(q_ref, kp_ref, vp_ref, kn_ref, vn_ref, o_ref, *, past):
    q = q_ref[...]
    s_past = _dot_nt(q, kp_ref[...])
    s_new = _dot_nt(q, kn_ref[...])
    s_past = jnp.where(_chunk_visible(past, 0, s_past.shape), s_past, NEG)
    s_new = jnp.where(_chunk_visible(past, past, s_new.shape), s_new, NEG)
    o_ref[...] = _two_part_softmax_pv(s_past, s_new, vp_ref[...], vn_ref[...]).astype(o_ref.dtype)


def _diff_sample_body(q_ref, kp_ref, vp_ref, kn_ref, vn_ref, slope_ref, lamv_ref, g_ref, o_ref,
                      *, past, lam_init):
    sq = q_ref.shape[0]
    qs = _stack_diff_queries(q_ref[...])
    slope = slope_ref[0][:, :1] * LOG2E

    def scores(k, k0):
        n = k.shape[0]
        s = _dot_nt(qs, k.astype(BF16))
        qpos = past + lax.broadcasted_iota(I32, (sq, n), 0)
        kpos = k0 + lax.broadcasted_iota(I32, (sq, n), 1)
        bias = -slope * jnp.abs(qpos - kpos).astype(F32)
        vis = _chunk_visible(past, k0, (sq, n))
        bias = jnp.concatenate([bias, bias], axis=0)
        vis = jnp.concatenate([vis, vis], axis=0)
        return jnp.where(vis, s + bias, NEG)

    o = _two_part_softmax_pv(scores(kp_ref[...], 0), scores(kn_ref[...], past),
                             vp_ref[...].astype(BF16), vn_ref[...])
    lam = _diff_lambda(lamv_ref, lam_init)
    o = o[:sq] - lam * o[sq:]
    o_ref[...] = (_rms_norm(o, g_ref[...], RMS_EPS) * (1.0 - lam_init)).astype(o_ref.dtype)


def _sample_attn_call(body, q, kp, vp, kn, vn, extra, extra_specs, *, batch, past, name):
    sq = q.shape[0] // batch
    assert past % CHUNK == 0 and kp.shape[0] == batch * past
    new = pl.BlockSpec((sq, LANE), lambda b, h: (b, h))
    old = pl.BlockSpec((past, LANE), lambda b, h: (b, h))
    return pl.pallas_call(
        functools.partial(body, past=past),
        out_shape=jax.ShapeDtypeStruct((batch * sq, HEAD_COLS), BF16),
        grid=(batch, H_A),
        in_specs=[new, old, old, new, new] + extra_specs,
        out_specs=new,
        compiler_params=_params(("parallel", "parallel")),
        name=name,
    )(q, kp, vp, kn, vn, *extra)


def _top4_of_row(logits):
    lane = lax.broadcasted_iota(I32, logits.shape, 1)
    vals = logits
    top_v, top_i = [], []
    for _ in range(TOP_K):
        mx = jnp.max(vals, axis=-1, keepdims=True)
        sel = jnp.min(jnp.where(vals == mx, lane, N_EXPERTS), axis=-1, keepdims=True)
        top_v.append(mx)
        top_i.append(sel)
        vals = jnp.where(lane == sel, -jnp.inf, vals)
    e = [jnp.exp(v - top_v[0]) for v in top_v]
    tot = e[0] + e[1] + e[2] + e[3]
    return top_i, [x / tot for x in e]


def _outproj_body(oa_ref, ob_ref, sga_ref, sgb_ref, x_ref, woa_ref, wob_ref, wout_ref, g_ref, b_ref,
                  wr_ref, br_ref, x1_ref, x1t_ref, idx_ref, gate_ref):
    a = _dot(oa_ref[...], woa_ref[...])
    b = _dot(ob_ref[...], wob_ref[...])
    merged = sga_ref[...].astype(F32) * a + sgb_ref[...].astype(F32) * b
    y = _dot(merged.astype(BF16), wout_ref[...])
    x1 = _layer_norm(ALPHA * x_ref[...] + y, g_ref[...], b_ref[...])
    x1_ref[...] = x1
    _store_row_tiles(x1t_ref, x1)
    x_hi = x1.astype(BF16)
    x_lo = (x1 - x_hi.astype(F32)).astype(BF16)
    hh_hl = _dot(x_hi, wr_ref[...])
    logits = (hh_hl[:, :N_EXPERTS] + hh_hl[:, N_EXPERTS:] + _dot(x_lo, wr_ref[:, :N_EXPERTS])) + br_ref[...]
    top_i, gates = _top4_of_row(logits)
    lane = lax.broadcasted_iota(I32, idx_ref.shape, 1)
    idx = jnp.zeros(idx_ref.shape, I32)
    gate = jnp.zeros(gate_ref.shape, F32)
    for k in range(TOP_K):
        idx = jnp.where(lane == k, top_i[k], idx)
        gate = jnp.where(lane == k, gates[k], gate)
    idx_ref[...] = idx
    gate_ref[...] = gate


def _outproj_call(oa, ob, sga, sgb, x, woa, wob, wout, g, b, wr, br):
    t, d = x.shape
    tm = _pick_tile(t, 256)
    row = lambda n: pl.BlockSpec((tm, n), lambda i: (i, 0))
    const = lambda a: pl.BlockSpec(a.shape, lambda i: (0, 0))
    return pl.pallas_call(
        _outproj_body,
        out_shape=(jax.ShapeDtypeStruct((t, d), F32), jax.ShapeDtypeStruct((t * ROW_TILE, LANE), F32),
                   jax.ShapeDtypeStruct((t, LANE), I32), jax.ShapeDtypeStruct((t, LANE), F32)),
        grid=(t // tm,),
        in_specs=[row(HEAD_COLS), row(HEAD_COLS), row(HEAD_COLS), row(HEAD_COLS), row(d),
                  const(woa), const(wob), const(wout), const(g), const(b), const(wr), const(br)],
        out_specs=(row(d), pl.BlockSpec((tm * ROW_TILE, LANE), lambda i: (i, 0)), row(LANE), row(LANE)),
        compiler_params=_params(("parallel",)),
        name="outproj",
    )(oa, ob, sga, sgb, x, woa, wob, wout, g, b, wr, br)


ROW_TILE = D_MODEL // LANE


def _store_row_tiles(ref, x, first=0):
    n = x.shape[0]
    for j in range(ROW_TILE):
        ref[pl.ds(first * ROW_TILE + j, n, stride=ROW_TILE), :] = x[:, j * LANE:(j + 1) * LANE]


def _load_row_tiles(ref, n, first=0):
    return jnp.concatenate(
        [ref[pl.ds(first * ROW_TILE + j, n, stride=ROW_TILE), :] for j in range(ROW_TILE)], axis=1)


def _row_tile(ref, r):
    return ref.at[pl.ds(pl.multiple_of(r * ROW_TILE, ROW_TILE), ROW_TILE), :]


def _rank_body(idx_ref, rank_ref, cnt_ref, run_sc):
    tr = idx_ref.shape[0]

    @pl.when(pl.program_id(0) == 0)
    def _():
        run_sc[...] = jnp.zeros(run_sc.shape, F32)

    idx = idx_ref[...]
    lane = lax.broadcasted_iota(I32, (tr, LANE), 1)
    chosen = [idx[:, k:k + 1] == lane for k in range(TOP_K)]
    member = functools.reduce(lambda a, b: a | b, chosen).astype(F32)
    below = (lax.broadcasted_iota(I32, (tr, tr), 0) > lax.broadcasted_iota(I32, (tr, tr), 1)).astype(BF16)
    before = _dot(below, member.astype(BF16)) + run_sc[...]
    rank = jnp.zeros((tr, LANE), F32)
    for k in range(TOP_K):
        rank = jnp.where(lane == k, jnp.sum(jnp.where(chosen[k], before, 0.0), axis=-1, keepdims=True), rank)
    rank_ref[...] = rank.astype(I32)
    run_sc[...] = run_sc[...] + jnp.sum(member, axis=0, keepdims=True)
    cnt_ref[...] = run_sc[...]


def _rank_call(top_idx):
    t = top_idx.shape[0]
    tr = _pick_tile(t, 256)
    row = pl.BlockSpec((tr, LANE), lambda i: (i, 0))
    return pl.pallas_call(
        _rank_body,
        out_shape=(jax.ShapeDtypeStruct((t, LANE), I32), jax.ShapeDtypeStruct((1, LANE), F32)),
        grid=(t // tr,),
        in_specs=[row],
        out_specs=(row, pl.BlockSpec((1, LANE), lambda i: (0, 0))),
        scratch_shapes=[pltpu.VMEM((1, LANE), F32)],
        compiler_params=_params(("arbitrary",)),
        name="moe_rank",
    )(top_idx)


def _route(top_idx, bm):
    t = top_idx.shape[0]
    rank, counts = _rank_call(top_idx)
    counts = counts[0, :N_EXPERTS].astype(I32)
    padded = (counts + bm - 1) // bm * bm
    pad_end = jnp.cumsum(padded)
    pad_start = pad_end - padded
    n_blocks = -(-t * TOP_K // bm) + N_EXPERTS
    block_e = jnp.minimum(
        jnp.sum(pad_end[None, :] <= (jnp.arange(n_blocks, dtype=I32) * bm)[:, None], axis=1), N_EXPERTS - 1).astype(I32)
    n_used = (pad_end[-1:] // bm).astype(I32)
    idx = top_idx[:, :TOP_K]
    start_of = jnp.sum(jnp.where(idx[:, :, None] == jnp.arange(N_EXPERTS, dtype=I32), pad_start, 0), axis=-1)
    return block_e, n_used, (start_of + rank[:, :TOP_K]).astype(I32)


DMA_UNROLL = 16


def _for_row_groups(n, fn):
    assert n % DMA_UNROLL == 0

    def body(g, c):
        for u in range(DMA_UNROLL):
            fn(g * DMA_UNROLL + u, u)
        return c

    lax.fori_loop(0, n // DMA_UNROLL, body, 0)


def _dispatch_body(dest_ref, x_ref, xs_in_hbm, xs_hbm, sem, *, td):
    del xs_in_hbm
    rows = TOP_K * td
    assert DMA_UNROLL % TOP_K == 0

    def start(r, u):
        tok = (r - u) // TOP_K + u // TOP_K
        pltpu.make_async_copy(_row_tile(x_ref, tok), _row_tile(xs_hbm, dest_ref[0, 0, r]), sem).start(priority=u % 2)

    def wait(r, u):
        pltpu.make_async_copy(_row_tile(x_ref, 0), _row_tile(xs_hbm, 0), sem).wait()

    _for_row_groups(rows, start)
    _for_row_groups(rows, wait)


def _dispatch_call(dest, x_tiles, xs_buf, *, td):
    t = x_tiles.shape[0] // ROW_TILE
    n = t // td
    rows = TOP_K * td
    return pl.pallas_call(
        functools.partial(_dispatch_body, td=td),
        out_shape=jax.ShapeDtypeStruct(xs_buf.shape, F32),
        grid=(n,),
        in_specs=[pl.BlockSpec((1, 1, rows), lambda i: (i, 0, 0), memory_space=pltpu.SMEM),
                  pl.BlockSpec((td * ROW_TILE, LANE), lambda i: (i, 0)), pl.BlockSpec(memory_space=pl.ANY)],
        out_specs=pl.BlockSpec(memory_space=pl.ANY),
        scratch_shapes=[pltpu.SemaphoreType.DMA],
        input_output_aliases={2: 0},
        compiler_params=_params(("arbitrary",)),
        name="moe_dispatch",
    )(dest.reshape(n, 1, rows), x_tiles, xs_buf)


def _moe_body(be_ref, nb_ref, xs_ref, wgu_ref, bgu_ref, wdn_ref, bdn_ref, y_ref):
    @pl.when(pl.program_id(0) < nb_ref[0])
    def _():
        x = _load_row_tiles(xs_ref, xs_ref.shape[0] // ROW_TILE).astype(BF16)
        h = _dot(x, wgu_ref[0]) + bgu_ref[0]
        g = jnp.minimum(h[:, :D_FF], SWIGLU_LIMIT)
        u = jnp.clip(h[:, D_FF:], -SWIGLU_LIMIT, SWIGLU_LIMIT)
        act = (u + 1.0) * (g * jax.nn.sigmoid(SWIGLU_ALPHA * g))
        _store_row_tiles(y_ref, _dot(act.astype(BF16), wdn_ref[0]) + bdn_ref[0])

    @pl.when(pl.program_id(0) >= nb_ref[0])
    def _():
        y_ref[...] = jnp.zeros(y_ref.shape, F32)


def _moe_call(block_e, n_used, xs, wgu, bgu, wdn, bdn, *, bm):
    n_blocks = block_e.shape[0]
    d = D_MODEL
    tiles = pl.BlockSpec((bm * ROW_TILE, LANE), lambda b, be, nb: (b, 0))
    grid_spec = pltpu.PrefetchScalarGridSpec(
        num_scalar_prefetch=2,
        grid=(n_blocks,),
        in_specs=[
            tiles,
            pl.BlockSpec((1, d, 2 * D_FF), lambda b, be, nb: (be[b], 0, 0)),
            pl.BlockSpec((1, 1, 2 * D_FF), lambda b, be, nb: (be[b], 0, 0)),
            pl.BlockSpec((1, D_FF, d), lambda b, be, nb: (be[b], 0, 0)),
            pl.BlockSpec((1, 1, d), lambda b, be, nb: (be[b], 0, 0)),
        ],
        out_specs=tiles,
    )
    return pl.pallas_call(
        _moe_body,
        out_shape=jax.ShapeDtypeStruct(xs.shape, F32),
        grid_spec=grid_spec,
        compiler_params=_params(("arbitrary",)),
        name="moe_experts",
    )(block_e, n_used, xs, wgu, bgu.reshape(N_EXPERTS, 1, 2 * D_FF), wdn, bdn.reshape(N_EXPERTS, 1, d))


def _start_row_gather(idx_ref, n, src_hbm, dst, sem):
    _for_row_groups(n, lambda r, u: pltpu.make_async_copy(
        _row_tile(src_hbm, idx_ref[0, 0, r]), _row_tile(dst, r), sem).start(priority=u % 2))


def _wait_row_gather(n, src_hbm, dst, sem):
    _for_row_groups(n, lambda r, u: pltpu.make_async_copy(_row_tile(src_hbm, 0), _row_tile(dst, r), sem).wait())


def _combine_body(dest_ref, destn_ref, y_hbm, x_ref, gate_ref, g_ref, b_ref, o_ref, ybuf, sem, *, tc):
    i = pl.program_id(0)
    n = pl.num_programs(0)
    slot = i % 2
    rows = TOP_K * tc

    @pl.when(i == 0)
    def _():
        _start_row_gather(dest_ref, rows, y_hbm, ybuf.at[0], sem.at[0])

    @pl.when(i + 1 < n)
    def _():
        _start_row_gather(destn_ref, rows, y_hbm, ybuf.at[1 - slot], sem.at[1 - slot])

    _wait_row_gather(rows, y_hbm, ybuf.at[slot], sem.at[slot])
    gate = gate_ref[...]
    moe = jnp.zeros(x_ref.shape, F32)
    for k in range(TOP_K):
        moe = moe + gate[:, k:k + 1] * _load_row_tiles(ybuf.at[slot], tc, first=k * tc)
    o_ref[...] = _layer_norm(ALPHA * x_ref[...] + moe, g_ref[...], b_ref[...])


def _combine_call(dest, y, x, gate, g, b, *, tc):
    t, d = x.shape
    n = t // tc
    rows = TOP_K * tc
    dest3 = dest.reshape(n, tc, TOP_K).transpose(0, 2, 1).reshape(n, 1, rows)
    smem_row = lambda f: pl.BlockSpec((1, 1, rows), f, memory_space=pltpu.SMEM)
    row = lambda w: pl.BlockSpec((tc, w), lambda i: (i, 0))
    const = lambda a: pl.BlockSpec(a.shape, lambda i: (0, 0))
    return pl.pallas_call(
        functools.partial(_combine_body, tc=tc),
        out_shape=jax.ShapeDtypeStruct((t, d), F32),
        grid=(n,),
        in_specs=[smem_row(lambda i: (i, 0, 0)), smem_row(lambda i: (jnp.minimum(i + 1, n - 1), 0, 0)),
                  pl.BlockSpec(memory_space=pl.ANY), row(d), row(LANE), const(g), const(b)],
        out_specs=row(d),
        scratch_shapes=[pltpu.VMEM((2, rows * ROW_TILE, LANE), F32), pltpu.SemaphoreType.DMA((2,))],
        compiler_params=_params(("arbitrary",)),
        name="moe_combine",
    )(dest3, dest3, y, x, gate, g, b)


def _head_blocks(w, widths):
    per = sum(widths)
    w = w.reshape(*w.shape[:-1], H_A, per)
    pad = [(0, 0)] * (w.ndim - 1) + [(0, LANE - per)]
    return jnp.pad(w, pad).reshape(*w.shape[:-2], HEAD_COLS)


def _rotate_half_cols(w):
    half = ROPE_DIM // 2
    return jnp.concatenate([-w[..., half:], w[..., :half]], axis=-1)


def _prep_layer(w_in, w_uq, w_ukv, w_oa):
    d = w_in.shape[0]
    splits = (Q_LORA, KV_LORA, ROPE_DIM, 1024, 1024, 1024, D_MODEL, D_MODEL)
    offs = [0]
    for s in splits:
        offs.append(offs[-1] + s)
    lat = w_in[:, offs[0]:offs[2]]
    kr = w_in[:, offs[2]:offs[3]]
    rest = w_in[:, offs[3]:]

    def rope_block(w):
        return jnp.pad(w, ((0, 0), (NOPE_DIM, LANE - NOPE_DIM - ROPE_DIM)))

    w1 = jnp.concatenate([lat, rope_block(kr), rope_block(_rotate_half_cols(kr)), rest], axis=1).astype(BF16)
    assert w1.shape == (d, C_END)
    wvt = w_in[:, offs[5]:offs[6]].T.astype(BF16)

    uq = w_uq.reshape(Q_LORA, H_A, NOPE_DIM + ROPE_DIM)
    uq_rot = jnp.concatenate([jnp.zeros((Q_LORA, H_A, NOPE_DIM), F32), _rotate_half_cols(uq[..., NOPE_DIM:])], -1)
    wuq = jnp.concatenate([_head_blocks(uq.reshape(Q_LORA, -1), (NOPE_DIM + ROPE_DIM,)),
                           _head_blocks(uq_rot.reshape(Q_LORA, -1), (NOPE_DIM + ROPE_DIM,))], axis=1).astype(BF16)

    ukv = w_ukv.reshape(KV_LORA, H_A, NOPE_DIM + V_DIM_A)
    wuk = _head_blocks(ukv[..., :NOPE_DIM].reshape(KV_LORA, -1), (NOPE_DIM,)).astype(BF16)
    wuv = _head_blocks(ukv[..., NOPE_DIM:].reshape(KV_LORA, -1), (V_DIM_A,)).astype(BF16)
    wuvt = ukv[..., NOPE_DIM:].reshape(KV_LORA, H_A * V_DIM_A).T.astype(BF16)

    woa = jnp.pad(w_oa.reshape(H_A, V_DIM_A, D_MODEL), ((0, 0), (0, LANE - V_DIM_A), (0, 0)))
    woa = woa.reshape(HEAD_COLS, D_MODEL).astype(BF16)
    return w1, wvt, wuq, wuk, wuv, wuvt, woa


def _rope_tables(pos):
    half = ROPE_DIM // 2
    inv = ROPE_THETA ** (-jnp.arange(half, dtype=F32) / half)
    ang = pos.astype(F32)[:, None] * inv[None, :]
    cos, sin = jnp.cos(ang), jnp.sin(ang)
    n = pos.shape[0]
    tail = jnp.zeros((n, LANE - NOPE_DIM - ROPE_DIM), F32)
    cos_t = jnp.concatenate([jnp.ones((n, NOPE_DIM), F32), cos, cos, tail], axis=1)
    sin_t = jnp.concatenate([jnp.zeros((n, NOPE_DIM), F32), sin, sin, tail], axis=1)
    return cos_t, sin_t


def _rope_block_of(kr):
    return jnp.pad(kr, ((0, 0), (NOPE_DIM, LANE - NOPE_DIM - ROPE_DIM)))


def _moe_block_rows(t):
    return 256 if t * TOP_K >= 256 * N_EXPERTS else 128


def _moe_row_buffer(t):
    bm = _moe_block_rows(t)
    return jnp.zeros(((-(-t * TOP_K // bm) + N_EXPERTS) * bm * ROW_TILE, LANE), F32)


def _moe_layer(x1, x1_tiles, top_idx, gate, rows_buf, wgu, bgu, wdn, bdn, g, b):
    t = x1.shape[0]
    bm = _moe_block_rows(t)
    tc = _pick_tile(t, 128)
    block_e, n_used, dest = _route(top_idx, bm)
    xs = _dispatch_call(dest, x1_tiles, rows_buf, td=_pick_tile(t, 512))
    y = _moe_call(block_e, n_used, xs, wgu, bgu, wdn, bdn, bm=bm)
    return _combine_call(dest, y, x1, gate, g, b, tc=tc), y


def kernel(x_prompt, x_sample, cache_mla_latent, cache_mla_krope, cache_diff_k, cache_diff_v,
           ln_in_g, ln_in_b, w_in, q_norm_g, kv_norm_g, w_uq, w_ukv, lam_q1, lam_k1, lam_q2, lam_k2,
           subln_g, w_oa, w_ob, w_out, ln1_g, ln1_b, w_router, b_router, w_gu, b_gu, w_dn, b_dn,
           ln2_g, ln2_b):
    bp, sp, d = x_prompt.shape
    bs, ss, _ = x_sample.shape
    past = cache_mla_latent.shape[2]
    assert bp == 1 and d == D_MODEL
    tp, ts = bp * sp, bs * ss

    cos_p, sin_p = _rope_tables(jnp.arange(sp, dtype=I32))
    cos_s, sin_s = _rope_tables(jnp.tile(past + jnp.arange(ss, dtype=I32), bs))
    slopes = 2.0 ** (-8.0 * (jnp.arange(H_B, dtype=F32) + 1.0) / H_B)
    slopes = jnp.broadcast_to(slopes[:, None, None], (H_B, 1, LANE))
    tk = _pick_tile(sp, 256)
    tq_mla = _pick_tile(sp, 512)
    tq_diff = _pick_tile(sp, 256)

    xp = _ln_call(x_prompt.reshape(tp, d), ln_in_g, ln_in_b)
    xs = _ln_call(x_sample.reshape(ts, d), ln_in_g, ln_in_b)
    rows_p, rows_s = _moe_row_buffer(tp), _moe_row_buffer(ts)
    new_p = ([], [], [], [])
    new_s = ([], [], [], [])
    for l in range(DEPTH):
        lam_init = 0.8 - 0.6 * math.exp(-0.3 * l)
        w1, wvt, wuq, wuk, wuv, wuvt, woa = _prep_layer(w_in[l], w_uq[l], w_ukv[l], w_oa[l])
        gq = q_norm_g[l].reshape(1, Q_LORA)
        gkv = kv_norm_g[l].reshape(1, KV_LORA)
        lamv = jnp.stack([lam_q1[l], lam_k1[l], lam_q2[l], lam_k2[l]])
        subg = subln_g[l].reshape(1, 2 * DH_B)
        wob = w_ob[l].astype(BF16)
        wout = w_out[l].astype(BF16)
        g1, b1 = ln1_g[l].reshape(1, d), ln1_b[l].reshape(1, d)
        g2, b2 = ln2_g[l].reshape(1, d), ln2_b[l].reshape(1, d)
        wr_hi = w_router[l].astype(BF16)
        wr = jnp.concatenate([wr_hi, (w_router[l] - wr_hi.astype(F32)).astype(BF16)], axis=1)
        br = b_router[l].reshape(1, N_EXPERTS)
        wgu, wdn = w_gu[l].astype(BF16), w_dn[l].astype(BF16)
        diff_extra = [slopes, lamv, subg]

        def diff_specs(head_of):
            return [pl.BlockSpec((1, 1, LANE), lambda a, b: (head_of(a, b), 0, 0)),
                    pl.BlockSpec(lamv.shape, lambda a, b: (0, 0)), pl.BlockSpec(subg.shape, lambda a, b: (0, 0))]

        qa, lat, krb, qd, kd32, kd16, vd32, _, vdt, sga, sgb = _inproj_call(xp, w1, wvt, wuq, gq, gkv, cos_p, sin_p)
        ka, vat = _kvup_call(lat, krb, wuk, wuvt, feature_major_v=True)
        oa = _prompt_attn_call(_mla_prompt_body, qa, ka, vat, [], [], maps=1, tq=tq_mla, tk=tk, unroll=MLA_UNROLL,
                               name="mla_prompt")
        ob = _prompt_attn_call(functools.partial(_diff_prompt_body, lam_init=lam_init), qd, kd16, vdt,
                               diff_extra, diff_specs(lambda h, i: h), maps=2, tq=tq_diff, tk=tk, unroll=DIFF_UNROLL,
                               name="diff_prompt")
        xp1, xp1t, idx_p, gate_p = _outproj_call(oa, ob, sga, sgb, xp, woa, wob, wout, g1, b1, wr, br)
        new_p[0].append(lat.reshape(bp, sp, KV_LORA))
        new_p[1].append(krb[:, NOPE_DIM:NOPE_DIM + ROPE_DIM].reshape(bp, sp, ROPE_DIM))
        new_p[2].append(kd32.reshape(bp, sp, 2 * H_B, DH_B))
        new_p[3].append(vd32.reshape(bp, sp, H_B, 2 * DH_B))

        qa, lat, krb, qd, kd32, kd16, vd32, vd16, _, sga, sgb = _inproj_call(xs, w1, wvt, wuq, gq, gkv, cos_s, sin_s)
        kn, vn = _kvup_call(lat, krb, wuk, wuv, feature_major_v=False)
        kpast, vpast = _kvup_call(cache_mla_latent[l].reshape(bs * past, KV_LORA),
                                  _rope_block_of(cache_mla_krope[l].reshape(bs * past, ROPE_DIM)), wuk, wuv,
                                  feature_major_v=False)
        oa = _sample_attn_call(_mla_sample_body, qa, kpast, vpast, kn, vn, [], [], batch=bs, past=past,
                               name="mla_sample")
        ob = _sample_attn_call(functools.partial(_diff_sample_body, lam_init=lam_init), qd,
                               cache_diff_k[l].reshape(bs * past, HEAD_COLS),
                               cache_diff_v[l].reshape(bs * past, HEAD_COLS), kd16, vd16,
                               diff_extra, diff_specs(lambda b, h: h), batch=bs, past=past, name="diff_sample")
        xs1, xs1t, idx_s, gate_s = _outproj_call(oa, ob, sga, sgb, xs, woa, wob, wout, g1, b1, wr, br)
        new_s[0].append(lat.reshape(bs, ss, KV_LORA))
        new_s[1].append(krb[:, NOPE_DIM:NOPE_DIM + ROPE_DIM].reshape(bs, ss, ROPE_DIM))
        new_s[2].append(kd32.reshape(bs, ss, 2 * H_B, DH_B))
        new_s[3].append(vd32.reshape(bs, ss, H_B, 2 * DH_B))

        xp, rows_p = _moe_layer(xp1, xp1t, idx_p, gate_p, rows_p, wgu, b_gu[l], wdn, b_dn[l], g2, b2)
        xs, rows_s = _moe_layer(xs1, xs1t, idx_s, gate_s, rows_s, wgu, b_gu[l], wdn, b_dn[l], g2, b2)

    return (xp.reshape(bp, sp, d), xs.reshape(bs, ss, d),
            jnp.stack(new_p[0]), jnp.stack(new_p[1]), jnp.stack(new_p[2]), jnp.stack(new_p[3]),
            jnp.stack(new_s[0]), jnp.stack(new_s[1]), jnp.stack(new_s[2]), jnp.stack(new_s[3]))
```

```python
import functools
import math

import jax
import jax.numpy as jnp
from jax import lax
from jax.experimental import pallas as pl
from jax.experimental.pallas import tpu as pltpu

F32 = jnp.float32
BF16 = jnp.bfloat16
I32 = jnp.int32

D_MODEL = 1024
DEPTH = 4
CHUNK = 64
CHUNK_SHIFT = 6
H_A = 8
NOPE_DIM = 64
ROPE_DIM = 32
V_DIM_A = 64
Q_LORA = 256
KV_LORA = 256
ROPE_THETA = 10000.0
H_B = 8
DH_B = 64
N_EXPERTS = 32
TOP_K = 4
D_FF = 1024
SWIGLU_LIMIT = 7.0
SWIGLU_ALPHA = 1.702
LN_EPS = 1e-5
RMS_EPS = 1e-6
ALPHA = (2 * DEPTH) ** 0.25

LANE = 128
BF16_ROWS = 16
HEAD_COLS = H_A * LANE
LOG2E = 1.4426950408889634
MLA_QSCALE = (NOPE_DIM + ROPE_DIM) ** -0.5 * LOG2E
DIFF_QSCALE = DH_B ** -0.5 * LOG2E
MLA_UNROLL = 8
DIFF_UNROLL = 8
NEG = -1e30
VMEM_LIMIT = 48 * 1024 * 1024

C_LAT = 0
C_KR = 512
C_KRROT = 640
C_QD = 768
C_KD = C_QD + 1024
C_VD = C_KD + 1024
C_GA = C_VD + 1024
C_GB = C_GA + 1024
C_END = C_GB + 1024


def _pick_tile(n, pref):
    t = min(n, pref)
    while n % t:
        t -= 8
    assert t > 0 and (t % 8 == 0 or t == n), (n, pref)
    return t


def _params(sem):
    return pltpu.CompilerParams(dimension_semantics=sem, vmem_limit_bytes=VMEM_LIMIT)


def _layer_norm(x, g, b):
    mu = jnp.mean(x, axis=-1, keepdims=True)
    xc = x - mu
    var = jnp.mean(xc * xc, axis=-1, keepdims=True)
    return xc * lax.rsqrt(var + LN_EPS) * g + b


def _rms_norm(x, g, eps):
    ms = jnp.mean(x * x, axis=-1, keepdims=True)
    return x * lax.rsqrt(ms + eps) * g


def _dot(a, b):
    return jnp.dot(a, b, preferred_element_type=F32)


def _dot_nt(a, b):
    return lax.dot_general(a, b, (((1,), (1,)), ((), ())), preferred_element_type=F32)


def _ln_body(x_ref, g_ref, b_ref, o_ref):
    o_ref[...] = _layer_norm(x_ref[...], g_ref[...], b_ref[...])


def _ln_call(x, g, b):
    t, d = x.shape
    tm = _pick_tile(t, 512)
    row = pl.BlockSpec((tm, d), lambda i: (i, 0))
    vec = pl.BlockSpec((1, d), lambda i: (0, 0))
    return pl.pallas_call(
        _ln_body,
        out_shape=jax.ShapeDtypeStruct((t, d), F32),
        grid=(t // tm,),
        in_specs=[row, vec, vec],
        out_specs=row,
        compiler_params=_params(("parallel",)),
        name="ln_in",
    )(x, g.reshape(1, d), b.reshape(1, d))


def _inproj_body(x_ref, w_ref, wvt_ref, wuq_ref, gq_ref, gkv_ref, cos_ref, sin_ref,
                 q_ref, lat_ref, krb_ref, qd_ref, kd32_ref, kd16_ref, vd32_ref, vd16_ref, vdt_ref,
                 sga_ref, sgb_ref):
    xb = x_ref[...].astype(BF16)
    cos = cos_ref[...]
    sin = sin_ref[...]

    def proj(lo, hi):
        return _dot(xb, w_ref[:, lo:hi])

    z = proj(C_LAT, C_QD)
    cq = _rms_norm(z[:, :Q_LORA], gq_ref[...], RMS_EPS)
    lat_ref[...] = _rms_norm(z[:, Q_LORA:C_KR], gkv_ref[...], RMS_EPS)
    krb_ref[...] = z[:, C_KR:C_KRROT] * cos + z[:, C_KRROT:C_QD] * sin
    qq = _dot(cq.astype(BF16), wuq_ref[...])
    for h in range(H_A):
        lo = h * LANE
        qh = qq[:, lo:lo + LANE] * cos + qq[:, HEAD_COLS + lo:HEAD_COLS + lo + LANE] * sin
        q_ref[:, lo:lo + LANE] = (qh * MLA_QSCALE).astype(BF16)
    qd_ref[...] = (proj(C_QD, C_KD) * DIFF_QSCALE).astype(BF16)
    z = proj(C_KD, C_VD)
    kd32_ref[...] = z
    kd16_ref[...] = z.astype(BF16)
    z = proj(C_VD, C_GA)
    vd32_ref[...] = z
    vd16_ref[...] = z.astype(BF16)
    vdt_ref[...] = _dot_nt(wvt_ref[...], xb).astype(BF16)
    sga_ref[...] = jax.nn.sigmoid(proj(C_GA, C_GB)).astype(BF16)
    sgb_ref[...] = jax.nn.sigmoid(proj(C_GB, C_END)).astype(BF16)


def _inproj_call(x, w, wvt, wuq, gq, gkv, cos_t, sin_t):
    t, d = x.shape
    tm = _pick_tile(t, 256)
    row = lambda n: pl.BlockSpec((tm, n), lambda i: (i, 0))
    const = lambda a: pl.BlockSpec(a.shape, lambda i: (0, 0), pipeline_mode=pl.Buffered(1))
    wide = lambda dt: jax.ShapeDtypeStruct((t, HEAD_COLS), dt)
    return pl.pallas_call(
        _inproj_body,
        out_shape=(wide(BF16), jax.ShapeDtypeStruct((t, KV_LORA), F32), jax.ShapeDtypeStruct((t, LANE), F32),
                   wide(BF16), wide(F32), wide(BF16), wide(F32), wide(BF16),
                   jax.ShapeDtypeStruct((HEAD_COLS, t), BF16), wide(BF16), wide(BF16)),
        grid=(t // tm,),
        in_specs=[row(d), const(w), const(wvt), const(wuq), const(gq), const(gkv), row(LANE), row(LANE)],
        out_specs=(row(HEAD_COLS), row(KV_LORA), row(LANE), row(HEAD_COLS), row(HEAD_COLS), row(HEAD_COLS),
                   row(HEAD_COLS), row(HEAD_COLS), pl.BlockSpec((HEAD_COLS, tm), lambda i: (0, i)),
                   row(HEAD_COLS), row(HEAD_COLS)),
        compiler_params=_params(("parallel",)),
        name="inproj",
    )(x, w, wvt, wuq, gq, gkv, cos_t, sin_t)


def _kvup_body(lat_ref, krb_ref, wk_ref, wv_ref, k_ref, v_ref, *, feature_major_v):
    lat = lat_ref[...].astype(BF16)
    kn = _dot(lat, wk_ref[...])
    krb = krb_ref[...]
    for h in range(H_A):
        lo = h * LANE
        k_ref[:, lo:lo + LANE] = (kn[:, lo:lo + LANE] + krb).astype(BF16)
    if feature_major_v:
        v_ref[...] = _dot_nt(wv_ref[...], lat).astype(BF16)
    else:
        v_ref[...] = _dot(lat, wv_ref[...]).astype(BF16)


def _kvup_call(lat, krb, wk, wv, *, feature_major_v):
    t = lat.shape[0]
    tm = _pick_tile(t, 512)
    row = lambda n: pl.BlockSpec((tm, n), lambda i: (i, 0))
    const = lambda a: pl.BlockSpec(a.shape, lambda i: (0, 0))
    if feature_major_v:
        v_shape, v_spec = (H_A * V_DIM_A, t), pl.BlockSpec((H_A * V_DIM_A, tm), lambda i: (0, i))
    else:
        v_shape, v_spec = (t, HEAD_COLS), row(HEAD_COLS)
    return pl.pallas_call(
        functools.partial(_kvup_body, feature_major_v=feature_major_v),
        out_shape=(jax.ShapeDtypeStruct((t, HEAD_COLS), BF16), jax.ShapeDtypeStruct(v_shape, BF16)),
        grid=(t // tm,),
        in_specs=[row(KV_LORA), row(LANE), const(wk), const(wv)],
        out_specs=(row(HEAD_COLS), v_spec),
        compiler_params=_params(("parallel",)),
        name="kvup",
    )(lat, krb, wk, wv)


def _stack_diff_queries(q):
    lane = lax.broadcasted_iota(I32, q.shape, 1)
    zero = jnp.zeros_like(q)
    return jnp.concatenate([jnp.where(lane < DH_B, q, zero), jnp.where(lane >= DH_B, q, zero)], axis=0)


def _diff_lambda(lamv_ref, lam_init):
    lv = lamv_ref[...]
    a = jnp.sum(lv[0:1] * lv[1:2], axis=-1, keepdims=True)
    b = jnp.sum(lv[2:3] * lv[3:4], axis=-1, keepdims=True)
    return jnp.exp(a) - jnp.exp(b) + lam_init


def _ones_row_block(tk):
    return (lax.broadcasted_iota(I32, (BF16_ROWS, tk), 0) == 0).astype(BF16)


def _visible_strip(q0, k0, tk):
    c = k0 + lax.broadcasted_iota(I32, (tk, LANE), 0)
    r = q0 + lax.broadcasted_iota(I32, (tk, LANE), 1)
    return (c >> CHUNK_SHIFT) <= (r >> CHUNK_SHIFT)


def _weights_and_rescale(s_ref, p_ref, m_prev, mx, offset):
    m_new = jnp.maximum(m_prev, mx + offset)
    p_ref[...] = jnp.exp2(s_ref[...] - (m_new - offset)).astype(BF16)
    return m_new, jnp.exp2(m_prev - m_new)


def _sweep_key_tiles(scores, values, offset, s_sc, p_sc, acc_sc, q0, tq, tk, unroll):
    width = s_sc.shape[2]
    acc_sc[...] = jnp.zeros(acc_sc.shape, F32)
    p_sc[1] = jnp.zeros(p_sc.shape[1:], BF16)
    n_full = q0 // tk
    s0 = scores(0, False)
    s_sc[0] = s0

    def step(j, cur, carry):
        m_prev, alpha_prev, mx = carry
        s_next = scores(j + 1, False)
        s_sc[1 - cur] = s_next
        acc_sc[...] = alpha_prev * acc_sc[...] + _dot(values(jnp.maximum(j - 1, 0)), p_sc[1 - cur])
        m_new, alpha = _weights_and_rescale(s_sc.at[cur], p_sc.at[cur], m_prev, mx, offset(j))
        return m_new, alpha, jnp.max(s_next, axis=0, keepdims=True)

    def steps(n, base, carry):
        for u in range(n):
            carry = step(base + u, u % 2, carry)
        return carry

    carry = (jnp.full((1, width), NEG, F32), jnp.ones((1, width), F32), jnp.max(s0, axis=0, keepdims=True))
    carry = lax.fori_loop(0, n_full // unroll, lambda jj, c: steps(unroll, unroll * jj, c), carry)
    n = unroll // 2
    while n:
        base = n_full // (2 * n) * (2 * n)
        carry = lax.fori_loop(0, (n_full // n) % 2, lambda _, c, n=n, base=base: steps(n, base, c), carry)
        n //= 2
    m, alpha, _ = carry
    last = (n_full + 1) % 2
    masked = [scores(n_full + j, True) for j in range(max(1, tq // tk))]
    acc_sc[...] = alpha * acc_sc[...] + _dot(values(jnp.maximum(n_full - 1, 0)), p_sc[last])
    for j, s in enumerate(masked):
        m_new = jnp.maximum(m, jnp.max(s, axis=0, keepdims=True))
        p = jnp.exp2(s - m_new).astype(BF16)
        acc_sc[...] = jnp.exp2(m - m_new) * acc_sc[...] + _dot(values(n_full + j), p)
        m = m_new
    return acc_sc[...]


def _mla_prompt_body(q_ref, k_ref, vt_ref, o_ref, s_sc, p_sc, acc_sc, *, tq, tk, unroll):
    q0 = pl.program_id(1) * tq
    q = q_ref[...]
    ones = _ones_row_block(tk)

    def key_start(kt):
        return pl.multiple_of(kt * tk, tk)

    def scores(kt, masked):
        s = _dot_nt(k_ref[pl.ds(key_start(kt), tk), :], q)
        if masked:
            vis = jnp.concatenate([_visible_strip(q0 + c * LANE, kt * tk, tk) for c in range(tq // LANE)], axis=1)
            s = jnp.where(vis, s, NEG)
        return s

    def values(kt):
        return jnp.concatenate([vt_ref[:, pl.ds(key_start(kt), tk)], ones], axis=0)

    acc = _sweep_key_tiles(scores, values, lambda kt: 0.0, s_sc, p_sc, acc_sc, q0, tq, tk, unroll)
    o = acc[:V_DIM_A] / acc[V_DIM_A:V_DIM_A + 1]
    o_ref[:, :V_DIM_A] = o.T.astype(o_ref.dtype)
    o_ref[:, V_DIM_A:] = jnp.zeros((tq, LANE - V_DIM_A), o_ref.dtype)


def _diff_prompt_body(q_ref, k_ref, vt_ref, slope_ref, lamv_ref, g_ref, o_ref, s_sc, p_sc, acc_sc,
                      *, tq, tk, unroll, lam_init):
    q0 = pl.program_id(1) * tq
    qs = _stack_diff_queries(q_ref[...])
    slope = slope_ref[0][:, :1] * LOG2E
    ones = _ones_row_block(tk)

    def key_start(kt):
        return pl.multiple_of(kt * tk, tk)

    row_bias = slope * lax.broadcasted_iota(I32, (tk, LANE), 0).astype(F32)

    def offset(kt):
        return slope * (kt * tk - q0).astype(F32)

    def scores(kt, masked):
        s = _dot_nt(k_ref[pl.ds(key_start(kt), tk), :], qs)
        if not masked:
            return s + jnp.concatenate([row_bias] * (2 * tq // LANE), axis=1)
        c_rel = (kt * tk - q0 + lax.broadcasted_iota(I32, (tk, LANE), 0)).astype(F32)
        strips = []
        for c in range(tq // LANE):
            a = (c * LANE + lax.broadcasted_iota(I32, (tk, LANE), 1)).astype(F32)
            bias = slope * jnp.minimum(c_rel, 2.0 * a - c_rel)
            strips.append(jnp.where(_visible_strip(q0 + c * LANE, kt * tk, tk), bias, NEG))
        return s + jnp.concatenate(strips + strips, axis=1)

    def values(kt):
        return jnp.concatenate([vt_ref[:, pl.ds(key_start(kt), tk)], ones], axis=0)

    acc = _sweep_key_tiles(scores, values, offset, s_sc, p_sc, acc_sc, q0, tq, tk, unroll)
    dv = 2 * DH_B
    o1 = acc[:dv, :tq] / acc[dv:dv + 1, :tq]
    o2 = acc[:dv, tq:] / acc[dv:dv + 1, tq:]
    o = o1 - _diff_lambda(lamv_ref, lam_init) * o2
    o = o * lax.rsqrt(jnp.mean(o * o, axis=0, keepdims=True) + RMS_EPS)
    o_ref[...] = (o.T * g_ref[...] * (1.0 - lam_init)).astype(o_ref.dtype)


def _prompt_attn_call(body, q, k, vt, extra, extra_specs, *, maps, tq, tk, unroll, name):
    t = q.shape[0]
    dv = vt.shape[0] // H_A
    width = maps * tq
    assert t % tq == 0 and t % tk == 0 and (tq % tk == 0 or tk % tq == 0) and tq % LANE == 0 and tk % CHUNK == 0
    qspec = pl.BlockSpec((tq, LANE), lambda h, i: (i, h))
    return pl.pallas_call(
        functools.partial(body, tq=tq, tk=tk, unroll=unroll),
        out_shape=jax.ShapeDtypeStruct((t, HEAD_COLS), BF16),
        grid=(H_A, t // tq),
        in_specs=[qspec, pl.BlockSpec((t, LANE), lambda h, i: (0, h)),
                  pl.BlockSpec((dv, t), lambda h, i: (h, 0))] + extra_specs,
        out_specs=qspec,
        scratch_shapes=[pltpu.VMEM((2, tk, width), F32), pltpu.VMEM((2, tk, width), BF16),
                        pltpu.VMEM((dv + BF16_ROWS, width), F32)],
        compiler_params=_params(("parallel", "arbitrary")),
        name=name,
    )(q, k, vt, *extra)


def _chunk_visible(q0, k0, shape):
    r = q0 + lax.broadcasted_iota(I32, shape, 0)
    c = k0 + lax.broadcasted_iota(I32, shape, 1)
    return (c >> CHUNK_SHIFT) <= (r >> CHUNK_SHIFT)


def _two_part_softmax_pv(s_past, s_new, v_past, v_new):
    m = jnp.maximum(jnp.max(s_past, axis=-1, keepdims=True), jnp.max(s_new, axis=-1, keepdims=True))
    p_past = jnp.exp2(s_past - m)
    p_new = jnp.exp2(s_new - m)
    l = jnp.sum(p_past, axis=-1, keepdims=True) + jnp.sum(p_new, axis=-1, keepdims=True)
    acc = _dot(p_past.astype(BF16), v_past) + _dot(p_new.astype(BF16), v_new)
    return acc / l


def _mla_sample_body(q_ref, kp_ref, vp_ref, kn_ref, vn_ref, o_ref, *, past):
    q = q_ref[...]
    s_past = _dot_nt(q, kp_ref[...])
    s_new = _dot_nt(q, kn_ref[...])
    s_past = jnp.where(_chunk_visible(past, 0, s_past.shape), s_past, NEG)
    s_new = jnp.where(_chunk_visible(past, past, s_new.shape), s_new, NEG)
    o_ref[...] = _two_part_softmax_pv(s_past, s_new, vp_ref[...], vn_ref[...]).astype(o_ref.dtype)


def _diff_sample_body(q_ref, kp_ref, vp_ref, kn_ref, vn_ref, slope_ref, lamv_ref, g_ref, o_ref,
                      *, past, lam_init):
    sq = q_ref.shape[0]
    qs = _stack_diff_queries(q_ref[...])
    slope = slope_ref[0][:, :1] * LOG2E

    def scores(s, k0):
        n = s.shape[1]
        qpos = past + lax.broadcasted_iota(I32, (sq, n), 0)
        kpos = k0 + lax.broadcasted_iota(I32, (sq, n), 1)
        bias = -slope * jnp.abs(qpos - kpos).astype(F32)
        vis = _chunk_visible(past, k0, (sq, n))
        bias = jnp.concatenate([bias, bias], axis=0)
        vis = jnp.concatenate([vis, vis], axis=0)
        return jnp.where(vis, s + bias, NEG)

    o = _two_part_softmax_pv(scores(_dot(qs, kp_ref[...].astype(BF16)), 0), scores(_dot_nt(qs, kn_ref[...]), past),
                             vp_ref[...].astype(BF16), vn_ref[...])
    lam = _diff_lambda(lamv_ref, lam_init)
    o = o[:sq] - lam * o[sq:]
    o_ref[...] = (_rms_norm(o, g_ref[...], RMS_EPS) * (1.0 - lam_init)).astype(o_ref.dtype)


def _sample_attn_call(body, q, kp, vp, kn, vn, extra, extra_specs, *, batch, past, kp_spec, vp_spec, name):
    sq = q.shape[0] // batch
    assert past % CHUNK == 0
    new = pl.BlockSpec((sq, LANE), lambda b, h: (b, h))
    return pl.pallas_call(
        functools.partial(body, past=past),
        out_shape=jax.ShapeDtypeStruct((batch * sq, HEAD_COLS), BF16),
        grid=(batch, H_A),
        in_specs=[new, kp_spec, vp_spec, new, new] + extra_specs,
        out_specs=new,
        compiler_params=_params(("parallel", "parallel")),
        name=name,
    )(q, kp, vp, kn, vn, *extra)


def _top4_of_row(logits):
    lane = lax.broadcasted_iota(I32, logits.shape, 1)
    vals = logits
    top_v, top_i = [], []
    for _ in range(TOP_K):
        mx = jnp.max(vals, axis=-1, keepdims=True)
        sel = jnp.min(jnp.where(vals == mx, lane, N_EXPERTS), axis=-1, keepdims=True)
        top_v.append(mx)
        top_i.append(sel)
        vals = jnp.where(lane == sel, -jnp.inf, vals)
    e = [jnp.exp(v - top_v[0]) for v in top_v]
    tot = e[0] + e[1] + e[2] + e[3]
    return top_i, [x / tot for x in e]


def _outproj_body(oa_ref, ob_ref, sga_ref, sgb_ref, x_ref, woa_ref, wob_ref, wout_ref, g_ref, b_ref,
                  wr_ref, br_ref, x1_ref, x1t_ref, idx_ref, gate_ref):
    a = _dot(oa_ref[...], woa_ref[...])
    b = _dot(ob_ref[...], wob_ref[...])
    merged = sga_ref[...].astype(F32) * a + sgb_ref[...].astype(F32) * b
    y = _dot(merged.astype(BF16), wout_ref[...])
    x1 = _layer_norm(ALPHA * x_ref[...] + y, g_ref[...], b_ref[...])
    x1_ref[...] = x1
    _store_row_tiles(x1t_ref, x1)
    x_hi = x1.astype(BF16)
    x_lo = (x1 - x_hi.astype(F32)).astype(BF16)
    hh_hl = _dot(x_hi, wr_ref[...])
    logits = (hh_hl[:, :N_EXPERTS] + hh_hl[:, N_EXPERTS:] + _dot(x_lo, wr_ref[:, :N_EXPERTS])) + br_ref[...]
    top_i, gates = _top4_of_row(logits)
    lane = lax.broadcasted_iota(I32, idx_ref.shape, 1)
    idx = jnp.zeros(idx_ref.shape, I32)
    gate = jnp.zeros(gate_ref.shape, F32)
    for k in range(TOP_K):
        idx = jnp.where(lane == k, top_i[k], idx)
        gate = jnp.where(lane == k, gates[k], gate)
    idx_ref[...] = idx
    gate_ref[...] = gate


def _outproj_call(oa, ob, sga, sgb, x, woa, wob, wout, g, b, wr, br):
    t, d = x.shape
    tm = _pick_tile(t, 256)
    row = lambda n: pl.BlockSpec((tm, n), lambda i: (i, 0))
    const = lambda a: pl.BlockSpec(a.shape, lambda i: (0, 0))
    return pl.pallas_call(
        _outproj_body,
        out_shape=(jax.ShapeDtypeStruct((t, d), F32), jax.ShapeDtypeStruct((t * ROW_TILE, LANE), F32),
                   jax.ShapeDtypeStruct((t, LANE), I32), jax.ShapeDtypeStruct((t, LANE), F32)),
        grid=(t // tm,),
        in_specs=[row(HEAD_COLS), row(HEAD_COLS), row(HEAD_COLS), row(HEAD_COLS), row(d),
                  const(woa), const(wob), const(wout), const(g), const(b), const(wr), const(br)],
        out_specs=(row(d), pl.BlockSpec((tm * ROW_TILE, LANE), lambda i: (i, 0)), row(LANE), row(LANE)),
        compiler_params=_params(("parallel",)),
        name="outproj",
    )(oa, ob, sga, sgb, x, woa, wob, wout, g, b, wr, br)


ROW_TILE = D_MODEL // LANE


def _store_row_tiles(ref, x, first=0):
    n = x.shape[0]
    for j in range(ROW_TILE):
        ref[pl.ds(first * ROW_TILE + j, n, stride=ROW_TILE), :] = x[:, j * LANE:(j + 1) * LANE]


def _load_row_tiles(ref, n, first=0):
    return jnp.concatenate(
        [ref[pl.ds(first * ROW_TILE + j, n, stride=ROW_TILE), :] for j in range(ROW_TILE)], axis=1)


def _row_tile(ref, r):
    return ref.at[pl.ds(pl.multiple_of(r * ROW_TILE, ROW_TILE), ROW_TILE), :]


def _rank_body(idx_ref, rank_ref, cnt_ref, run_sc):
    tr = idx_ref.shape[0]

    @pl.when(pl.program_id(0) == 0)
    def _():
        run_sc[...] = jnp.zeros(run_sc.shape, F32)

    idx = idx_ref[...]
    lane = lax.broadcasted_iota(I32, (tr, LANE), 1)
    chosen = [idx[:, k:k + 1] == lane for k in range(TOP_K)]
    member = functools.reduce(lambda a, b: a | b, chosen).astype(F32)
    below = (lax.broadcasted_iota(I32, (tr, tr), 0) > lax.broadcasted_iota(I32, (tr, tr), 1)).astype(BF16)
    before = _dot(below, member.astype(BF16)) + run_sc[...]
    rank = jnp.zeros((tr, LANE), F32)
    for k in range(TOP_K):
        rank = jnp.where(lane == k, jnp.sum(jnp.where(chosen[k], before, 0.0), axis=-1, keepdims=True), rank)
    rank_ref[...] = rank.astype(I32)
    run_sc[...] = run_sc[...] + jnp.sum(member, axis=0, keepdims=True)
    cnt_ref[...] = run_sc[...]


def _rank_call(top_idx):
    t = top_idx.shape[0]
    tr = _pick_tile(t, 256)
    row = pl.BlockSpec((tr, LANE), lambda i: (i, 0))
    return pl.pallas_call(
        _rank_body,
        out_shape=(jax.ShapeDtypeStruct((t, LANE), I32), jax.ShapeDtypeStruct((1, LANE), F32)),
        grid=(t // tr,),
        in_specs=[row],
        out_specs=(row, pl.BlockSpec((1, LANE), lambda i: (0, 0))),
        scratch_shapes=[pltpu.VMEM((1, LANE), F32)],
        compiler_params=_params(("arbitrary",)),
        name="moe_rank",
    )(top_idx)


def _route(top_idx, bm):
    t = top_idx.shape[0]
    rank, counts = _rank_call(top_idx)
    counts = counts[0, :N_EXPERTS].astype(I32)
    padded = (counts + bm - 1) // bm * bm
    pad_end = jnp.cumsum(padded)
    pad_start = pad_end - padded
    n_blocks = -(-t * TOP_K // bm) + N_EXPERTS
    block_e = jnp.minimum(
        jnp.sum(pad_end[None, :] <= (jnp.arange(n_blocks, dtype=I32) * bm)[:, None], axis=1), N_EXPERTS - 1).astype(I32)
    n_used = (pad_end[-1:] // bm).astype(I32)
    idx = top_idx[:, :TOP_K]
    start_of = jnp.sum(jnp.where(idx[:, :, None] == jnp.arange(N_EXPERTS, dtype=I32), pad_start, 0), axis=-1)
    return block_e, n_used, (start_of + rank[:, :TOP_K]).astype(I32)


DMA_UNROLL = 16


def _for_row_groups(n, fn):
    assert n % DMA_UNROLL == 0

    def body(g, c):
        for u in range(DMA_UNROLL):
            fn(g * DMA_UNROLL + u, u)
        return c

    lax.fori_loop(0, n // DMA_UNROLL, body, 0)


def _dispatch_body(dest_ref, x_ref, xs_in_hbm, xs_hbm, sem, *, td):
    del xs_in_hbm
    rows = TOP_K * td
    assert DMA_UNROLL % TOP_K == 0

    def start(r, u):
        tok = (r - u) // TOP_K + u // TOP_K
        pltpu.make_async_copy(_row_tile(x_ref, tok), _row_tile(xs_hbm, dest_ref[0, 0, r]), sem).start(priority=u % 2)

    def wait(r, u):
        pltpu.make_async_copy(_row_tile(x_ref, 0), _row_tile(xs_hbm, 0), sem).wait()

    _for_row_groups(rows, start)
    _for_row_groups(rows, wait)


def _dispatch_call(dest, x_tiles, xs_buf, *, td):
    t = x_tiles.shape[0] // ROW_TILE
    n = t // td
    rows = TOP_K * td
    return pl.pallas_call(
        functools.partial(_dispatch_body, td=td),
        out_shape=jax.ShapeDtypeStruct(xs_buf.shape, F32),
        grid=(n,),
        in_specs=[pl.BlockSpec((1, 1, rows), lambda i: (i, 0, 0), memory_space=pltpu.SMEM),
                  pl.BlockSpec((td * ROW_TILE, LANE), lambda i: (i, 0)), pl.BlockSpec(memory_space=pl.ANY)],
        out_specs=pl.BlockSpec(memory_space=pl.ANY),
        scratch_shapes=[pltpu.SemaphoreType.DMA],
        input_output_aliases={2: 0},
        compiler_params=_params(("arbitrary",)),
        name="moe_dispatch",
    )(dest.reshape(n, 1, rows), x_tiles, xs_buf)


def _moe_body(be_ref, nb_ref, xs_ref, wgu_ref, bgu_ref, wdn_ref, bdn_ref, y_ref):
    @pl.when(pl.program_id(0) < nb_ref[0])
    def _():
        x = _load_row_tiles(xs_ref, xs_ref.shape[0] // ROW_TILE).astype(BF16)
        h = _dot(x, wgu_ref[0]) + bgu_ref[0]
        g = jnp.minimum(h[:, :D_FF], SWIGLU_LIMIT)
        u = jnp.clip(h[:, D_FF:], -SWIGLU_LIMIT, SWIGLU_LIMIT)
        act = (u + 1.0) * (g * jax.nn.sigmoid(SWIGLU_ALPHA * g))
        _store_row_tiles(y_ref, _dot(act.astype(BF16), wdn_ref[0]) + bdn_ref[0])

    @pl.when(pl.program_id(0) >= nb_ref[0])
    def _():
        y_ref[...] = jnp.zeros(y_ref.shape, F32)


def _moe_call(block_e, n_used, xs, wgu, bgu, wdn, bdn, *, layer, bm):
    n_blocks = block_e.shape[0]
    n_layers = wgu.shape[0]
    d = D_MODEL
    tiles = pl.BlockSpec((bm * ROW_TILE, LANE), lambda b, be, nb: (b, 0))
    grid_spec = pltpu.PrefetchScalarGridSpec(
        num_scalar_prefetch=2,
        grid=(n_blocks,),
        in_specs=[
            tiles,
            pl.BlockSpec((None, 1, d, 2 * D_FF), lambda b, be, nb: (layer, be[b], 0, 0)),
            pl.BlockSpec((None, 1, 1, 2 * D_FF), lambda b, be, nb: (layer, be[b], 0, 0)),
            pl.BlockSpec((None, 1, D_FF, d), lambda b, be, nb: (layer, be[b], 0, 0)),
            pl.BlockSpec((None, 1, 1, d), lambda b, be, nb: (layer, be[b], 0, 0)),
        ],
        out_specs=tiles,
    )
    return pl.pallas_call(
        _moe_body,
        out_shape=jax.ShapeDtypeStruct(xs.shape, F32),
        grid_spec=grid_spec,
        compiler_params=_params(("arbitrary",)),
        name="moe_experts",
    )(block_e, n_used, xs, wgu, bgu.reshape(n_layers, N_EXPERTS, 1, 2 * D_FF), wdn,
      bdn.reshape(n_layers, N_EXPERTS, 1, d))


def _start_row_gather(idx_ref, n, src_hbm, dst, sem):
    _for_row_groups(n, lambda r, u: pltpu.make_async_copy(
        _row_tile(src_hbm, idx_ref[0, 0, r]), _row_tile(dst, r), sem).start(priority=u % 2))


def _wait_row_gather(n, src_hbm, dst, sem):
    _for_row_groups(n, lambda r, u: pltpu.make_async_copy(_row_tile(src_hbm, 0), _row_tile(dst, r), sem).wait())


def _combine_body(dest_ref, destn_ref, y_hbm, x_ref, gate_ref, g_ref, b_ref, o_ref, ybuf, sem, *, tc):
    i = pl.program_id(0)
    n = pl.num_programs(0)
    slot = i % 2
    rows = TOP_K * tc

    @pl.when(i == 0)
    def _():
        _start_row_gather(dest_ref, rows, y_hbm, ybuf.at[0], sem.at[0])

    @pl.when(i + 1 < n)
    def _():
        _start_row_gather(destn_ref, rows, y_hbm, ybuf.at[1 - slot], sem.at[1 - slot])

    _wait_row_gather(rows, y_hbm, ybuf.at[slot], sem.at[slot])
    gate = gate_ref[...]
    moe = jnp.zeros(x_ref.shape, F32)
    for k in range(TOP_K):
        moe = moe + gate[:, k:k + 1] * _load_row_tiles(ybuf.at[slot], tc, first=k * tc)
    o_ref[...] = _layer_norm(ALPHA * x_ref[...] + moe, g_ref[...], b_ref[...])


def _combine_call(dest, y, x, gate, g, b, *, tc):
    t, d = x.shape
    n = t // tc
    rows = TOP_K * tc
    dest3 = dest.reshape(n, tc, TOP_K).transpose(0, 2, 1).reshape(n, 1, rows)
    smem_row = lambda f: pl.BlockSpec((1, 1, rows), f, memory_space=pltpu.SMEM)
    row = lambda w: pl.BlockSpec((tc, w), lambda i: (i, 0))
    const = lambda a: pl.BlockSpec(a.shape, lambda i: (0, 0))
    return pl.pallas_call(
        functools.partial(_combine_body, tc=tc),
        out_shape=jax.ShapeDtypeStruct((t, d), F32),
        grid=(n,),
        in_specs=[smem_row(lambda i: (i, 0, 0)), smem_row(lambda i: (jnp.minimum(i + 1, n - 1), 0, 0)),
                  pl.BlockSpec(memory_space=pl.ANY), row(d), row(LANE), const(g), const(b)],
        out_specs=row(d),
        scratch_shapes=[pltpu.VMEM((2, rows * ROW_TILE, LANE), F32), pltpu.SemaphoreType.DMA((2,))],
        compiler_params=_params(("arbitrary",)),
        name="moe_combine",
    )(dest3, dest3, y, x, gate, g, b)


def _head_blocks(w, widths):
    per = sum(widths)
    w = w.reshape(*w.shape[:-1], H_A, per)
    pad = [(0, 0)] * (w.ndim - 1) + [(0, LANE - per)]
    return jnp.pad(w, pad).reshape(*w.shape[:-2], HEAD_COLS)


def _rotate_half_cols(w):
    half = ROPE_DIM // 2
    return jnp.concatenate([-w[..., half:], w[..., :half]], axis=-1)


def _prep_layer(w_in, w_uq, w_ukv, w_oa):
    d = w_in.shape[0]
    splits = (Q_LORA, KV_LORA, ROPE_DIM, 1024, 1024, 1024, D_MODEL, D_MODEL)
    offs = [0]
    for s in splits:
        offs.append(offs[-1] + s)
    lat = w_in[:, offs[0]:offs[2]]
    kr = w_in[:, offs[2]:offs[3]]
    rest = w_in[:, offs[3]:]

    def rope_block(w):
        return jnp.pad(w, ((0, 0), (NOPE_DIM, LANE - NOPE_DIM - ROPE_DIM)))

    w1 = jnp.concatenate([lat, rope_block(kr), rope_block(_rotate_half_cols(kr)), rest], axis=1).astype(BF16)
    assert w1.shape == (d, C_END)
    wvt = w_in[:, offs[5]:offs[6]].T.astype(BF16)

    uq = w_uq.reshape(Q_LORA, H_A, NOPE_DIM + ROPE_DIM)
    uq_rot = jnp.concatenate([jnp.zeros((Q_LORA, H_A, NOPE_DIM), F32), _rotate_half_cols(uq[..., NOPE_DIM:])], -1)
    wuq = jnp.concatenate([_head_blocks(uq.reshape(Q_LORA, -1), (NOPE_DIM + ROPE_DIM,)),
                           _head_blocks(uq_rot.reshape(Q_LORA, -1), (NOPE_DIM + ROPE_DIM,))], axis=1).astype(BF16)

    ukv = w_ukv.reshape(KV_LORA, H_A, NOPE_DIM + V_DIM_A)
    wuk = _head_blocks(ukv[..., :NOPE_DIM].reshape(KV_LORA, -1), (NOPE_DIM,)).astype(BF16)
    wuv = _head_blocks(ukv[..., NOPE_DIM:].reshape(KV_LORA, -1), (V_DIM_A,)).astype(BF16)
    wuvt = ukv[..., NOPE_DIM:].reshape(KV_LORA, H_A * V_DIM_A).T.astype(BF16)

    woa = jnp.pad(w_oa.reshape(H_A, V_DIM_A, D_MODEL), ((0, 0), (0, LANE - V_DIM_A), (0, 0)))
    woa = woa.reshape(HEAD_COLS, D_MODEL).astype(BF16)
    return w1, wvt, wuq, wuk, wuv, wuvt, woa


def _rope_tables(pos):
    half = ROPE_DIM // 2
    inv = ROPE_THETA ** (-jnp.arange(half, dtype=F32) / half)
    ang = pos.astype(F32)[:, None] * inv[None, :]
    cos, sin = jnp.cos(ang), jnp.sin(ang)
    n = pos.shape[0]
    tail = jnp.zeros((n, LANE - NOPE_DIM - ROPE_DIM), F32)
    cos_t = jnp.concatenate([jnp.ones((n, NOPE_DIM), F32), cos, cos, tail], axis=1)
    sin_t = jnp.concatenate([jnp.zeros((n, NOPE_DIM), F32), sin, sin, tail], axis=1)
    return cos_t, sin_t


def _rope_block_of(kr):
    return jnp.pad(kr, ((0, 0), (NOPE_DIM, LANE - NOPE_DIM - ROPE_DIM)))


def _moe_block_rows(t):
    return 256 if t * TOP_K >= 256 * N_EXPERTS else 128


def _moe_row_buffer(t):
    bm = _moe_block_rows(t)
    return jnp.zeros(((-(-t * TOP_K // bm) + N_EXPERTS) * bm * ROW_TILE, LANE), F32)


def _moe_layer(x1, x1_tiles, top_idx, gate, rows_buf, wgu, bgu, wdn, bdn, g, b, *, layer):
    t = x1.shape[0]
    bm = _moe_block_rows(t)
    tc = _pick_tile(t, 128)
    block_e, n_used, dest = _route(top_idx, bm)
    xs = _dispatch_call(dest, x1_tiles, rows_buf, td=_pick_tile(t, 512))
    y = _moe_call(block_e, n_used, xs, wgu, bgu, wdn, bdn, layer=layer, bm=bm)
    return _combine_call(dest, y, x1, gate, g, b, tc=tc), y


def kernel(x_prompt, x_sample, cache_mla_latent, cache_mla_krope, cache_diff_k, cache_diff_v,
           ln_in_g, ln_in_b, w_in, q_norm_g, kv_norm_g, w_uq, w_ukv, lam_q1, lam_k1, lam_q2, lam_k2,
           subln_g, w_oa, w_ob, w_out, ln1_g, ln1_b, w_router, b_router, w_gu, b_gu, w_dn, b_dn,
           ln2_g, ln2_b):
    bp, sp, d = x_prompt.shape
    bs, ss, _ = x_sample.shape
    past = cache_mla_latent.shape[2]
    assert bp == 1 and d == D_MODEL
    tp, ts = bp * sp, bs * ss

    cos_p, sin_p = _rope_tables(jnp.arange(sp, dtype=I32))
    cos_s, sin_s = _rope_tables(jnp.tile(past + jnp.arange(ss, dtype=I32), bs))
    slopes = 2.0 ** (-8.0 * (jnp.arange(H_B, dtype=F32) + 1.0) / H_B)
    slopes = jnp.broadcast_to(slopes[:, None, None], (H_B, 1, LANE))
    tk = _pick_tile(sp, 256)
    tq_mla = _pick_tile(sp, 512)
    tq_diff = _pick_tile(sp, 256)

    xp = _ln_call(x_prompt.reshape(tp, d), ln_in_g, ln_in_b)
    xs = _ln_call(x_sample.reshape(ts, d), ln_in_g, ln_in_b)
    rows_p, rows_s = _moe_row_buffer(tp), _moe_row_buffer(ts)
    wgu, wdn = w_gu.astype(BF16), w_dn.astype(BF16)
    diff_kt = jnp.transpose(cache_diff_k, (0, 1, 3, 4, 2)).reshape(DEPTH * bs * HEAD_COLS, past)
    diff_v = cache_diff_v.reshape(DEPTH * bs * past, HEAD_COLS)
    new_p = ([], [], [], [])
    new_s = ([], [], [], [])
    for l in range(DEPTH):
        lam_init = 0.8 - 0.6 * math.exp(-0.3 * l)
        w1, wvt, wuq, wuk, wuv, wuvt, woa = _prep_layer(w_in[l], w_uq[l], w_ukv[l], w_oa[l])
        gq = q_norm_g[l].reshape(1, Q_LORA)
        gkv = kv_norm_g[l].reshape(1, KV_LORA)
        lamv = jnp.stack([lam_q1[l], lam_k1[l], lam_q2[l], lam_k2[l]])
        subg = subln_g[l].reshape(1, 2 * DH_B)
        wob = w_ob[l].astype(BF16)
        wout = w_out[l].astype(BF16)
        g1, b1 = ln1_g[l].reshape(1, d), ln1_b[l].reshape(1, d)
        g2, b2 = ln2_g[l].reshape(1, d), ln2_b[l].reshape(1, d)
        wr_hi = w_router[l].astype(BF16)
        wr = jnp.concatenate([wr_hi, (w_router[l] - wr_hi.astype(F32)).astype(BF16)], axis=1)
        br = b_router[l].reshape(1, N_EXPERTS)
        diff_extra = [slopes, lamv, subg]

        def diff_specs(head_of):
            return [pl.BlockSpec((1, 1, LANE), lambda a, b: (head_of(a, b), 0, 0)),
                    pl.BlockSpec(lamv.shape, lambda a, b: (0, 0)), pl.BlockSpec(subg.shape, lambda a, b: (0, 0))]

        qa, lat, krb, qd, kd32, kd16, vd32, _, vdt, sga, sgb = _inproj_call(xp, w1, wvt, wuq, gq, gkv, cos_p, sin_p)
        ka, vat = _kvup_call(lat, krb, wuk, wuvt, feature_major_v=True)
        oa = _prompt_attn_call(_mla_prompt_body, qa, ka, vat, [], [], maps=1, tq=tq_mla, tk=tk, unroll=MLA_UNROLL,
                               name="mla_prompt")
        ob = _prompt_attn_call(functools.partial(_diff_prompt_body, lam_init=lam_init), qd, kd16, vdt,
                               diff_extra, diff_specs(lambda h, i: h), maps=2, tq=tq_diff, tk=tk, unroll=DIFF_UNROLL,
                               name="diff_prompt")
        xp1, xp1t, idx_p, gate_p = _outproj_call(oa, ob, sga, sgb, xp, woa, wob, wout, g1, b1, wr, br)
        new_p[0].append(lat.reshape(bp, sp, KV_LORA))
        new_p[1].append(krb[:, NOPE_DIM:NOPE_DIM + ROPE_DIM].reshape(bp, sp, ROPE_DIM))
        new_p[2].append(kd32.reshape(bp, sp, 2 * H_B, DH_B))
        new_p[3].append(vd32.reshape(bp, sp, H_B, 2 * DH_B))

        qa, lat, krb, qd, kd32, kd16, vd32, vd16, _, sga, sgb = _inproj_call(xs, w1, wvt, wuq, gq, gkv, cos_s, sin_s)
        kn, vn = _kvup_call(lat, krb, wuk, wuv, feature_major_v=False)
        kpast, vpast = _kvup_call(cache_mla_latent[l].reshape(bs * past, KV_LORA),
                                  _rope_block_of(cache_mla_krope[l].reshape(bs * past, ROPE_DIM)), wuk, wuv,
                                  feature_major_v=False)
        token_major = pl.BlockSpec((past, LANE), lambda b, h: (b, h))
        oa = _sample_attn_call(_mla_sample_body, qa, kpast, vpast, kn, vn, [], [], batch=bs, past=past,
                               kp_spec=token_major, vp_spec=token_major, name="mla_sample")
        ob = _sample_attn_call(functools.partial(_diff_sample_body, lam_init=lam_init), qd, diff_kt, diff_v, kd16, vd16,
                               diff_extra, diff_specs(lambda b, h: h), batch=bs, past=past,
                               kp_spec=pl.BlockSpec((LANE, past), lambda b, h, l=l: ((l * bs + b) * H_B + h, 0)),
                               vp_spec=pl.BlockSpec((past, LANE), lambda b, h, l=l: (l * bs + b, h)),
                               name="diff_sample")
        xs1, xs1t, idx_s, gate_s = _outproj_call(oa, ob, sga, sgb, xs, woa, wob, wout, g1, b1, wr, br)
        new_s[0].append(lat.reshape(bs, ss, KV_LORA))
        new_s[1].append(krb[:, NOPE_DIM:NOPE_DIM + ROPE_DIM].reshape(bs, ss, ROPE_DIM))
        new_s[2].append(kd32.reshape(bs, ss, 2 * H_B, DH_B))
        new_s[3].append(vd32.reshape(bs, ss, H_B, 2 * DH_B))

        xp, rows_p = _moe_layer(xp1, xp1t, idx_p, gate_p, rows_p, wgu, b_gu, wdn, b_dn, g2, b2, layer=l)
        xs, rows_s = _moe_layer(xs1, xs1t, idx_s, gate_s, rows_s, wgu, b_gu, wdn, b_dn, g2, b2, layer=l)

    return (xp.reshape(bp, sp, d), xs.reshape(bs, ss, d),
            jnp.stack(new_p[0]), jnp.stack(new_p[1]), jnp.stack(new_p[2]), jnp.stack(new_p[3]),
            jnp.stack(new_s[0]), jnp.stack(new_s[1]), jnp.stack(new_s[2]), jnp.stack(new_s[3]))
```

```python
import functools
import math

import jax
import jax.numpy as jnp
from jax import lax
from jax.experimental import pallas as pl
from jax.experimental.pallas import tpu as pltpu

F32 = jnp.float32
BF16 = jnp.bfloat16
I32 = jnp.int32

D_MODEL = 1024
DEPTH = 4
CHUNK = 64
CHUNK_SHIFT = 6
H_A = 8
NOPE_DIM = 64
ROPE_DIM = 32
V_DIM_A = 64
Q_LORA = 256
KV_LORA = 256
ROPE_THETA = 10000.0
H_B = 8
DH_B = 64
N_EXPERTS = 32
TOP_K = 4
D_FF = 1024
SWIGLU_LIMIT = 7.0
SWIGLU_ALPHA = 1.702
LN_EPS = 1e-5
RMS_EPS = 1e-6
ALPHA = (2 * DEPTH) ** 0.25

LANE = 128
BF16_ROWS = 16
HEAD_COLS = H_A * LANE
LOG2E = 1.4426950408889634
MLA_QSCALE = (NOPE_DIM + ROPE_DIM) ** -0.5 * LOG2E
DIFF_QSCALE = DH_B ** -0.5 * LOG2E
MLA_UNROLL = 16
DIFF_UNROLL = 16
NEG = -1e30
VMEM_LIMIT = 48 * 1024 * 1024

C_LAT = 0
C_KR = 512
C_KRROT = 640
C_QD = 768
C_KD = C_QD + 1024
C_VD = C_KD + 1024
C_GA = C_VD + 1024
C_GB = C_GA + 1024
C_END = C_GB + 1024


def _pick_tile(n, pref):
    t = min(n, pref)
    while n % t:
        t -= 8
    assert t > 0 and (t % 8 == 0 or t == n), (n, pref)
    return t


def _params(sem):
    return pltpu.CompilerParams(dimension_semantics=sem, vmem_limit_bytes=VMEM_LIMIT)


def _layer_norm(x, g, b):
    mu = jnp.mean(x, axis=-1, keepdims=True)
    xc = x - mu
    var = jnp.mean(xc * xc, axis=-1, keepdims=True)
    return xc * lax.rsqrt(var + LN_EPS) * g + b


def _rms_norm(x, g, eps):
    ms = jnp.mean(x * x, axis=-1, keepdims=True)
    return x * lax.rsqrt(ms + eps) * g


def _dot(a, b):
    return jnp.dot(a, b, preferred_element_type=F32)


def _dot_nt(a, b):
    return lax.dot_general(a, b, (((1,), (1,)), ((), ())), preferred_element_type=F32)


def _ln_body(x_ref, g_ref, b_ref, o_ref):
    o_ref[...] = _layer_norm(x_ref[...], g_ref[...], b_ref[...])


def _ln_call(x, g, b):
    t, d = x.shape
    tm = _pick_tile(t, 512)
    row = pl.BlockSpec((tm, d), lambda i: (i, 0))
    vec = pl.BlockSpec((1, d), lambda i: (0, 0))
    return pl.pallas_call(
        _ln_body,
        out_shape=jax.ShapeDtypeStruct((t, d), F32),
        grid=(t // tm,),
        in_specs=[row, vec, vec],
        out_specs=row,
        compiler_params=_params(("parallel",)),
        name="ln_in",
    )(x, g.reshape(1, d), b.reshape(1, d))


def _inproj_body(x_ref, w_ref, wvt_ref, wuq_ref, gq_ref, gkv_ref, cos_ref, sin_ref,
                 q_ref, lat_ref, krb_ref, qd_ref, kd32_ref, kd16_ref, vd32_ref, vd16_ref, vdt_ref,
                 sga_ref, sgb_ref):
    xb = x_ref[...].astype(BF16)
    cos = cos_ref[...]
    sin = sin_ref[...]

    def proj(lo, hi):
        return _dot(xb, w_ref[:, lo:hi])

    z = proj(C_LAT, C_QD)
    cq = _rms_norm(z[:, :Q_LORA], gq_ref[...], RMS_EPS)
    lat_ref[...] = _rms_norm(z[:, Q_LORA:C_KR], gkv_ref[...], RMS_EPS)
    krb_ref[...] = z[:, C_KR:C_KRROT] * cos + z[:, C_KRROT:C_QD] * sin
    qq = _dot(cq.astype(BF16), wuq_ref[...])
    for h in range(H_A):
        lo = h * LANE
        qh = qq[:, lo:lo + LANE] * cos + qq[:, HEAD_COLS + lo:HEAD_COLS + lo + LANE] * sin
        q_ref[:, lo:lo + LANE] = (qh * MLA_QSCALE).astype(BF16)
    qd_ref[...] = (proj(C_QD, C_KD) * DIFF_QSCALE).astype(BF16)
    z = proj(C_KD, C_VD)
    kd32_ref[...] = z
    kd16_ref[...] = z.astype(BF16)
    z = proj(C_VD, C_GA)
    vd32_ref[...] = z
    vd16_ref[...] = z.astype(BF16)
    vdt_ref[...] = _dot_nt(wvt_ref[...], xb).astype(BF16)
    sga_ref[...] = jax.nn.sigmoid(proj(C_GA, C_GB)).astype(BF16)
    sgb_ref[...] = jax.nn.sigmoid(proj(C_GB, C_END)).astype(BF16)


def _inproj_call(x, w, wvt, wuq, gq, gkv, cos_t, sin_t):
    t, d = x.shape
    tm = _pick_tile(t, 256)
    row = lambda n: pl.BlockSpec((tm, n), lambda i: (i, 0))
    const = lambda a: pl.BlockSpec(a.shape, lambda i: (0, 0), pipeline_mode=pl.Buffered(1))
    wide = lambda dt: jax.ShapeDtypeStruct((t, HEAD_COLS), dt)
    return pl.pallas_call(
        _inproj_body,
        out_shape=(wide(BF16), jax.ShapeDtypeStruct((t, KV_LORA), F32), jax.ShapeDtypeStruct((t, LANE), F32),
                   wide(BF16), wide(F32), wide(BF16), wide(F32), wide(BF16),
                   jax.ShapeDtypeStruct((HEAD_COLS, t), BF16), wide(BF16), wide(BF16)),
        grid=(t // tm,),
        in_specs=[row(d), const(w), const(wvt), const(wuq), const(gq), const(gkv), row(LANE), row(LANE)],
        out_specs=(row(HEAD_COLS), row(KV_LORA), row(LANE), row(HEAD_COLS), row(HEAD_COLS), row(HEAD_COLS),
                   row(HEAD_COLS), row(HEAD_COLS), pl.BlockSpec((HEAD_COLS, tm), lambda i: (0, i)),
                   row(HEAD_COLS), row(HEAD_COLS)),
        compiler_params=_params(("parallel",)),
        name="inproj",
    )(x, w, wvt, wuq, gq, gkv, cos_t, sin_t)


def _kvup_body(lat_ref, krb_ref, wk_ref, wv_ref, k_ref, v_ref, *, feature_major_v):
    lat = lat_ref[...].astype(BF16)
    kn = _dot(lat, wk_ref[...])
    krb = krb_ref[...]
    for h in range(H_A):
        lo = h * LANE
        k_ref[:, lo:lo + LANE] = (kn[:, lo:lo + LANE] + krb).astype(BF16)
    if feature_major_v:
        v_ref[...] = _dot_nt(wv_ref[...], lat).astype(BF16)
    else:
        v_ref[...] = _dot(lat, wv_ref[...]).astype(BF16)


def _kvup_call(lat, krb, wk, wv, *, feature_major_v, lat_first_row=0):
    t = krb.shape[0]
    tm = _pick_tile(t, 512)
    assert lat_first_row % tm == 0
    row = lambda n: pl.BlockSpec((tm, n), lambda i: (i, 0))
    const = lambda a: pl.BlockSpec(a.shape, lambda i: (0, 0))
    if feature_major_v:
        v_shape, v_spec = (H_A * V_DIM_A, t), pl.BlockSpec((H_A * V_DIM_A, tm), lambda i: (0, i))
    else:
        v_shape, v_spec = (t, HEAD_COLS), row(HEAD_COLS)
    return pl.pallas_call(
        functools.partial(_kvup_body, feature_major_v=feature_major_v),
        out_shape=(jax.ShapeDtypeStruct((t, HEAD_COLS), BF16), jax.ShapeDtypeStruct(v_shape, BF16)),
        grid=(t // tm,),
        in_specs=[pl.BlockSpec((tm, KV_LORA), lambda i: (i + lat_first_row // tm, 0)), row(LANE), const(wk), const(wv)],
        out_specs=(row(HEAD_COLS), v_spec),
        compiler_params=_params(("parallel",)),
        name="kvup",
    )(lat, krb, wk, wv)


def _stack_diff_queries(q):
    lane = lax.broadcasted_iota(I32, q.shape, 1)
    zero = jnp.zeros_like(q)
    return jnp.concatenate([jnp.where(lane < DH_B, q, zero), jnp.where(lane >= DH_B, q, zero)], axis=0)


def _diff_lambda(lamv_ref, lam_init):
    lv = lamv_ref[...]
    a = jnp.sum(lv[0:1] * lv[1:2], axis=-1, keepdims=True)
    b = jnp.sum(lv[2:3] * lv[3:4], axis=-1, keepdims=True)
    return jnp.exp(a) - jnp.exp(b) + lam_init


def _ones_row_block(tk):
    return (lax.broadcasted_iota(I32, (BF16_ROWS, tk), 0) == 0).astype(BF16)


def _visible_strip(q0, k0, tk):
    c = k0 + lax.broadcasted_iota(I32, (tk, LANE), 0)
    r = q0 + lax.broadcasted_iota(I32, (tk, LANE), 1)
    return (c >> CHUNK_SHIFT) <= (r >> CHUNK_SHIFT)


def _weights_and_rescale(s_ref, p_ref, m_prev, mx, offset):
    m_new = jnp.maximum(m_prev, mx + offset)
    p_ref[...] = jnp.exp2(s_ref[...] - (m_new - offset)).astype(BF16)
    return m_new, jnp.exp2(m_prev - m_new)


def _sweep_key_tiles(scores, values, offset, s_sc, p_sc, acc_sc, q0, tq, tk, unroll):
    width = s_sc.shape[2]
    acc_sc[...] = jnp.zeros(acc_sc.shape, F32)
    p_sc[1] = jnp.zeros(p_sc.shape[1:], BF16)
    n_full = q0 // tk
    s0 = scores(0, False)
    s_sc[0] = s0

    def step(j, cur, carry):
        m_prev, alpha_prev, mx = carry
        s_next = scores(j + 1, False)
        s_sc[1 - cur] = s_next
        acc_sc[...] = alpha_prev * acc_sc[...] + _dot(values(jnp.maximum(j - 1, 0)), p_sc[1 - cur])
        m_new, alpha = _weights_and_rescale(s_sc.at[cur], p_sc.at[cur], m_prev, mx, offset(j))
        return m_new, alpha, jnp.max(s_next, axis=0, keepdims=True)

    def steps(n, base, carry):
        for u in range(n):
            carry = step(base + u, u % 2, carry)
        return carry

    carry = (jnp.full((1, width), NEG, F32), jnp.ones((1, width), F32), jnp.max(s0, axis=0, keepdims=True))
    carry = lax.fori_loop(0, n_full // unroll, lambda jj, c: steps(unroll, unroll * jj, c), carry)
    n = unroll // 2
    while n:
        base = n_full // (2 * n) * (2 * n)
        carry = lax.fori_loop(0, (n_full // n) % 2, lambda _, c, n=n, base=base: steps(n, base, c), carry)
        n //= 2
    m, alpha, _ = carry
    last = (n_full + 1) % 2
    masked = [scores(n_full + j, True) for j in range(max(1, tq // tk))]
    acc_sc[...] = alpha * acc_sc[...] + _dot(values(jnp.maximum(n_full - 1, 0)), p_sc[last])
    for j, s in enumerate(masked):
        m_new = jnp.maximum(m, jnp.max(s, axis=0, keepdims=True))
        p = jnp.exp2(s - m_new).astype(BF16)
        acc_sc[...] = jnp.exp2(m - m_new) * acc_sc[...] + _dot(values(n_full + j), p)
        m = m_new
    return acc_sc[...]


def _mla_prompt_body(q_ref, k_ref, vt_ref, o_ref, s_sc, p_sc, acc_sc, *, tq, tk, unroll):
    q0 = pl.program_id(1) * tq
    q = q_ref[...]
    ones = _ones_row_block(tk)

    def key_start(kt):
        return pl.multiple_of(kt * tk, tk)

    def scores(kt, masked):
        s = _dot_nt(k_ref[pl.ds(key_start(kt), tk), :], q)
        if masked:
            vis = jnp.concatenate([_visible_strip(q0 + c * LANE, kt * tk, tk) for c in range(tq // LANE)], axis=1)
            s = jnp.where(vis, s, NEG)
        return s

    def values(kt):
        return jnp.concatenate([vt_ref[:, pl.ds(key_start(kt), tk)], ones], axis=0)

    acc = _sweep_key_tiles(scores, values, lambda kt: 0.0, s_sc, p_sc, acc_sc, q0, tq, tk, unroll)
    o = acc[:V_DIM_A] / acc[V_DIM_A:V_DIM_A + 1]
    o_ref[:, :V_DIM_A] = o.T.astype(o_ref.dtype)
    o_ref[:, V_DIM_A:] = jnp.zeros((tq, LANE - V_DIM_A), o_ref.dtype)


def _diff_prompt_body(q_ref, k_ref, vt_ref, slope_ref, lamv_ref, g_ref, o_ref, s_sc, p_sc, acc_sc,
                      *, tq, tk, unroll, lam_init):
    q0 = pl.program_id(1) * tq
    qs = _stack_diff_queries(q_ref[...])
    slope = slope_ref[0][:, :1] * LOG2E
    ones = _ones_row_block(tk)

    def key_start(kt):
        return pl.multiple_of(kt * tk, tk)

    row_bias = slope * lax.broadcasted_iota(I32, (tk, LANE), 0).astype(F32)

    def offset(kt):
        return slope * (kt * tk - q0).astype(F32)

    def scores(kt, masked):
        s = _dot_nt(k_ref[pl.ds(key_start(kt), tk), :], qs)
        if not masked:
            return s + jnp.concatenate([row_bias] * (2 * tq // LANE), axis=1)
        c_rel = (kt * tk - q0 + lax.broadcasted_iota(I32, (tk, LANE), 0)).astype(F32)
        strips = []
        for c in range(tq // LANE):
            a = (c * LANE + lax.broadcasted_iota(I32, (tk, LANE), 1)).astype(F32)
            bias = slope * jnp.minimum(c_rel, 2.0 * a - c_rel)
            strips.append(jnp.where(_visible_strip(q0 + c * LANE, kt * tk, tk), bias, NEG))
        return s + jnp.concatenate(strips + strips, axis=1)

    def values(kt):
        return jnp.concatenate([vt_ref[:, pl.ds(key_start(kt), tk)], ones], axis=0)

    acc = _sweep_key_tiles(scores, values, offset, s_sc, p_sc, acc_sc, q0, tq, tk, unroll)
    dv = 2 * DH_B
    o1 = acc[:dv, :tq] / acc[dv:dv + 1, :tq]
    o2 = acc[:dv, tq:] / acc[dv:dv + 1, tq:]
    o = o1 - _diff_lambda(lamv_ref, lam_init) * o2
    o = o * lax.rsqrt(jnp.mean(o * o, axis=0, keepdims=True) + RMS_EPS)
    o_ref[...] = (o.T * g_ref[...] * (1.0 - lam_init)).astype(o_ref.dtype)


def _prompt_attn_call(body, q, k, vt, extra, extra_specs, *, maps, tq, tk, unroll, name):
    t = q.shape[0]
    dv = vt.shape[0] // H_A
    width = maps * tq
    assert t % tq == 0 and t % tk == 0 and (tq % tk == 0 or tk % tq == 0) and tq % LANE == 0 and tk % CHUNK == 0
    qspec = pl.BlockSpec((tq, LANE), lambda h, i: (i, h))
    return pl.pallas_call(
        functools.partial(body, tq=tq, tk=tk, unroll=unroll),
        out_shape=jax.ShapeDtypeStruct((t, HEAD_COLS), BF16),
        grid=(H_A, t // tq),
        in_specs=[qspec, pl.BlockSpec((t, LANE), lambda h, i: (0, h)),
                  pl.BlockSpec((dv, t), lambda h, i: (h, 0))] + extra_specs,
        out_specs=qspec,
        scratch_shapes=[pltpu.VMEM((2, tk, width), F32), pltpu.VMEM((2, tk, width), BF16),
                        pltpu.VMEM((dv + BF16_ROWS, width), F32)],
        compiler_params=_params(("parallel", "arbitrary")),
        name=name,
    )(q, k, vt, *extra)


def _chunk_visible(q0, k0, shape):
    r = q0 + lax.broadcasted_iota(I32, shape, 0)
    c = k0 + lax.broadcasted_iota(I32, shape, 1)
    return (c >> CHUNK_SHIFT) <= (r >> CHUNK_SHIFT)


def _two_part_softmax_pv(s_past, s_new, v_past, v_new):
    m = jnp.maximum(jnp.max(s_past, axis=-1, keepdims=True), jnp.max(s_new, axis=-1, keepdims=True))
    p_past = jnp.exp2(s_past - m)
    p_new = jnp.exp2(s_new - m)
    l = jnp.sum(p_past, axis=-1, keepdims=True) + jnp.sum(p_new, axis=-1, keepdims=True)
    acc = _dot(p_past.astype(BF16), v_past) + _dot(p_new.astype(BF16), v_new)
    return acc / l


def _mla_sample_body(q_ref, kp_ref, vp_ref, kn_ref, vn_ref, o_ref, *, past):
    q = q_ref[...]
    s_past = _dot_nt(q, kp_ref[...])
    s_new = _dot_nt(q, kn_ref[...])
    s_past = jnp.where(_chunk_visible(past, 0, s_past.shape), s_past, NEG)
    s_new = jnp.where(_chunk_visible(past, past, s_new.shape), s_new, NEG)
    o_ref[...] = _two_part_softmax_pv(s_past, s_new, vp_ref[...], vn_ref[...]).astype(o_ref.dtype)


def _diff_sample_body(q_ref, kp_ref, vp_ref, kn_ref, vn_ref, slope_ref, lamv_ref, g_ref, o_ref,
                      *, past, lam_init):
    sq = q_ref.shape[0]
    qs = _stack_diff_queries(q_ref[...])
    slope = slope_ref[0][:, :1] * LOG2E

    def scores(s, k0):
        n = s.shape[1]
        qpos = past + lax.broadcasted_iota(I32, (sq, n), 0)
        kpos = k0 + lax.broadcasted_iota(I32, (sq, n), 1)
        bias = -slope * jnp.abs(qpos - kpos).astype(F32)
        vis = _chunk_visible(past, k0, (sq, n))
        bias = jnp.concatenate([bias, bias], axis=0)
        vis = jnp.concatenate([vis, vis], axis=0)
        return jnp.where(vis, s + bias, NEG)

    o = _two_part_softmax_pv(scores(_dot(qs, kp_ref[...].astype(BF16)), 0), scores(_dot_nt(qs, kn_ref[...]), past),
                             vp_ref[...].astype(BF16), vn_ref[...])
    lam = _diff_lambda(lamv_ref, lam_init)
    o = o[:sq] - lam * o[sq:]
    o_ref[...] = (_rms_norm(o, g_ref[...], RMS_EPS) * (1.0 - lam_init)).astype(o_ref.dtype)


def _sample_attn_call(body, q, kp, vp, kn, vn, extra, extra_specs, *, batch, past, kp_spec, vp_spec, name):
    sq = q.shape[0] // batch
    assert past % CHUNK == 0
    new = pl.BlockSpec((sq, LANE), lambda b, h: (b, h))
    return pl.pallas_call(
        functools.partial(body, past=past),
        out_shape=jax.ShapeDtypeStruct((batch * sq, HEAD_COLS), BF16),
        grid=(batch, H_A),
        in_specs=[new, kp_spec, vp_spec, new, new] + extra_specs,
        out_specs=new,
        compiler_params=_params(("parallel", "parallel")),
        name=name,
    )(q, kp, vp, kn, vn, *extra)


def _top4_of_row(logits):
    lane = lax.broadcasted_iota(I32, logits.shape, 1)
    vals = logits
    top_v, top_i = [], []
    for _ in range(TOP_K):
        mx = jnp.max(vals, axis=-1, keepdims=True)
        sel = jnp.min(jnp.where(vals == mx, lane, N_EXPERTS), axis=-1, keepdims=True)
        top_v.append(mx)
        top_i.append(sel)
        vals = jnp.where(lane == sel, -jnp.inf, vals)
    e = [jnp.exp(v - top_v[0]) for v in top_v]
    tot = e[0] + e[1] + e[2] + e[3]
    return top_i, [x / tot for x in e]


def _outproj_body(oa_ref, ob_ref, sga_ref, sgb_ref, x_ref, woa_ref, wob_ref, wout_ref, g_ref, b_ref,
                  wr_ref, br_ref, x1_ref, x1t_ref, idx_ref, gate_ref):
    a = _dot(oa_ref[...], woa_ref[...])
    b = _dot(ob_ref[...], wob_ref[...])
    merged = sga_ref[...].astype(F32) * a + sgb_ref[...].astype(F32) * b
    y = _dot(merged.astype(BF16), wout_ref[...])
    x1 = _layer_norm(ALPHA * x_ref[...] + y, g_ref[...], b_ref[...])
    x1_ref[...] = x1
    _store_row_tiles(x1t_ref, x1)
    x_hi = x1.astype(BF16)
    x_lo = (x1 - x_hi.astype(F32)).astype(BF16)
    hh_hl = _dot(x_hi, wr_ref[...])
    logits = (hh_hl[:, :N_EXPERTS] + hh_hl[:, N_EXPERTS:] + _dot(x_lo, wr_ref[:, :N_EXPERTS])) + br_ref[...]
    top_i, gates = _top4_of_row(logits)
    lane = lax.broadcasted_iota(I32, idx_ref.shape, 1)
    idx = jnp.zeros(idx_ref.shape, I32)
    gate = jnp.zeros(gate_ref.shape, F32)
    for k in range(TOP_K):
        idx = jnp.where(lane == k, top_i[k], idx)
        gate = jnp.where(lane == k, gates[k], gate)
    idx_ref[...] = idx
    gate_ref[...] = gate


def _outproj_call(oa, ob, sga, sgb, x, woa, wob, wout, g, b, wr, br):
    t, d = x.shape
    tm = _pick_tile(t, 256)
    row = lambda n: pl.BlockSpec((tm, n), lambda i: (i, 0))
    const = lambda a: pl.BlockSpec(a.shape, lambda i: (0, 0))
    return pl.pallas_call(
        _outproj_body,
        out_shape=(jax.ShapeDtypeStruct((t, d), F32), jax.ShapeDtypeStruct((t * ROW_TILE, LANE), F32),
                   jax.ShapeDtypeStruct((t, LANE), I32), jax.ShapeDtypeStruct((t, LANE), F32)),
        grid=(t // tm,),
        in_specs=[row(HEAD_COLS), row(HEAD_COLS), row(HEAD_COLS), row(HEAD_COLS), row(d),
                  const(woa), const(wob), const(wout), const(g), const(b), const(wr), const(br)],
        out_specs=(row(d), pl.BlockSpec((tm * ROW_TILE, LANE), lambda i: (i, 0)), row(LANE), row(LANE)),
        compiler_params=_params(("parallel",)),
        name="outproj",
    )(oa, ob, sga, sgb, x, woa, wob, wout, g, b, wr, br)


ROW_TILE = D_MODEL // LANE


def _store_row_tiles(ref, x, first=0):
    n = x.shape[0]
    for j in range(ROW_TILE):
        ref[pl.ds(first * ROW_TILE + j, n, stride=ROW_TILE), :] = x[:, j * LANE:(j + 1) * LANE]


def _load_row_tiles(ref, n, first=0):
    return jnp.concatenate(
        [ref[pl.ds(first * ROW_TILE + j, n, stride=ROW_TILE), :] for j in range(ROW_TILE)], axis=1)


def _row_tile(ref, r):
    return ref.at[pl.ds(pl.multiple_of(r * ROW_TILE, ROW_TILE), ROW_TILE), :]


def _rank_body(idx_ref, rank_ref, cnt_ref, run_sc):
    tr = idx_ref.shape[0]

    @pl.when(pl.program_id(0) == 0)
    def _():
        run_sc[...] = jnp.zeros(run_sc.shape, F32)

    idx = idx_ref[...]
    lane = lax.broadcasted_iota(I32, (tr, LANE), 1)
    chosen = [idx[:, k:k + 1] == lane for k in range(TOP_K)]
    member = functools.reduce(lambda a, b: a | b, chosen).astype(F32)
    below = (lax.broadcasted_iota(I32, (tr, tr), 0) > lax.broadcasted_iota(I32, (tr, tr), 1)).astype(BF16)
    before = _dot(below, member.astype(BF16)) + run_sc[...]
    rank = jnp.zeros((tr, LANE), F32)
    for k in range(TOP_K):
        rank = jnp.where(lane == k, jnp.sum(jnp.where(chosen[k], before, 0.0), axis=-1, keepdims=True), rank)
    rank_ref[...] = rank.astype(I32)
    run_sc[...] = run_sc[...] + jnp.sum(member, axis=0, keepdims=True)
    cnt_ref[...] = run_sc[...]


def _rank_call(top_idx):
    t = top_idx.shape[0]
    tr = _pick_tile(t, 256)
    row = pl.BlockSpec((tr, LANE), lambda i: (i, 0))
    return pl.pallas_call(
        _rank_body,
        out_shape=(jax.ShapeDtypeStruct((t, LANE), I32), jax.ShapeDtypeStruct((1, LANE), F32)),
        grid=(t // tr,),
        in_specs=[row],
        out_specs=(row, pl.BlockSpec((1, LANE), lambda i: (0, 0))),
        scratch_shapes=[pltpu.VMEM((1, LANE), F32)],
        compiler_params=_params(("arbitrary",)),
        name="moe_rank",
    )(top_idx)


def _route(top_idx, bm):
    t = top_idx.shape[0]
    rank, counts = _rank_call(top_idx)
    counts = counts[0, :N_EXPERTS].astype(I32)
    padded = (counts + bm - 1) // bm * bm
    pad_end = jnp.cumsum(padded)
    pad_start = pad_end - padded
    n_blocks = -(-t * TOP_K // bm) + N_EXPERTS
    block_e = jnp.minimum(
        jnp.sum(pad_end[None, :] <= (jnp.arange(n_blocks, dtype=I32) * bm)[:, None], axis=1), N_EXPERTS - 1).astype(I32)
    n_used = (pad_end[-1:] // bm).astype(I32)
    idx = top_idx[:, :TOP_K]
    start_of = jnp.sum(jnp.where(idx[:, :, None] == jnp.arange(N_EXPERTS, dtype=I32), pad_start, 0), axis=-1)
    return block_e, n_used, (start_of + rank[:, :TOP_K]).astype(I32)


DMA_UNROLL = 16


def _for_row_groups(n, fn):
    assert n % DMA_UNROLL == 0

    def body(g, c):
        for u in range(DMA_UNROLL):
            fn(g * DMA_UNROLL + u, u)
        return c

    lax.fori_loop(0, n // DMA_UNROLL, body, 0)


def _dispatch_body(dest_ref, x_ref, xs_in_hbm, xs_hbm, sem, *, td):
    del xs_in_hbm
    rows = TOP_K * td
    assert DMA_UNROLL % TOP_K == 0

    def start(r, u):
        tok = (r - u) // TOP_K + u // TOP_K
        pltpu.make_async_copy(_row_tile(x_ref, tok), _row_tile(xs_hbm, dest_ref[0, 0, r]), sem).start(priority=u % 2)

    def wait(r, u):
        pltpu.make_async_copy(_row_tile(x_ref, 0), _row_tile(xs_hbm, 0), sem).wait()

    _for_row_groups(rows, start)
    _for_row_groups(rows, wait)


def _dispatch_call(dest, x_tiles, xs_buf, *, td):
    t = x_tiles.shape[0] // ROW_TILE
    n = t // td
    rows = TOP_K * td
    return pl.pallas_call(
        functools.partial(_dispatch_body, td=td),
        out_shape=jax.ShapeDtypeStruct(xs_buf.shape, F32),
        grid=(n,),
        in_specs=[pl.BlockSpec((1, 1, rows), lambda i: (i, 0, 0), memory_space=pltpu.SMEM),
                  pl.BlockSpec((td * ROW_TILE, LANE), lambda i: (i, 0)), pl.BlockSpec(memory_space=pl.ANY)],
        out_specs=pl.BlockSpec(memory_space=pl.ANY),
        scratch_shapes=[pltpu.SemaphoreType.DMA],
        input_output_aliases={2: 0},
        compiler_params=_params(("arbitrary",)),
        name="moe_dispatch",
    )(dest.reshape(n, 1, rows), x_tiles, xs_buf)


def _moe_body(be_ref, nb_ref, xs_ref, wgu_ref, bgu_ref, wdn_ref, bdn_ref, y_ref):
    @pl.when(pl.program_id(0) < nb_ref[0])
    def _():
        x = _load_row_tiles(xs_ref, xs_ref.shape[0] // ROW_TILE).astype(BF16)
        h = _dot(x, wgu_ref[0]) + bgu_ref[0]
        g = jnp.minimum(h[:, :D_FF], SWIGLU_LIMIT)
        u = jnp.clip(h[:, D_FF:], -SWIGLU_LIMIT, SWIGLU_LIMIT)
        act = (u + 1.0) * (g * jax.nn.sigmoid(SWIGLU_ALPHA * g))
        _store_row_tiles(y_ref, _dot(act.astype(BF16), wdn_ref[0]) + bdn_ref[0])

    @pl.when(pl.program_id(0) >= nb_ref[0])
    def _():
        y_ref[...] = jnp.zeros(y_ref.shape, F32)


def _moe_call(block_e, n_used, xs, wgu, bgu, wdn, bdn, *, layer, bm):
    n_blocks = block_e.shape[0]
    n_layers = wgu.shape[0]
    d = D_MODEL
    tiles = pl.BlockSpec((bm * ROW_TILE, LANE), lambda b, be, nb: (b, 0))
    grid_spec = pltpu.PrefetchScalarGridSpec(
        num_scalar_prefetch=2,
        grid=(n_blocks,),
        in_specs=[
            tiles,
            pl.BlockSpec((None, 1, d, 2 * D_FF), lambda b, be, nb: (layer, be[b], 0, 0)),
            pl.BlockSpec((None, 1, 1, 2 * D_FF), lambda b, be, nb: (layer, be[b], 0, 0)),
            pl.BlockSpec((None, 1, D_FF, d), lambda b, be, nb: (layer, be[b], 0, 0)),
            pl.BlockSpec((None, 1, 1, d), lambda b, be, nb: (layer, be[b], 0, 0)),
        ],
        out_specs=tiles,
    )
    return pl.pallas_call(
        _moe_body,
        out_shape=jax.ShapeDtypeStruct(xs.shape, F32),
        grid_spec=grid_spec,
        compiler_params=_params(("arbitrary",)),
        name="moe_experts",
    )(block_e, n_used, xs, wgu, bgu.reshape(n_layers, N_EXPERTS, 1, 2 * D_FF), wdn,
      bdn.reshape(n_layers, N_EXPERTS, 1, d))


def _start_row_gather(idx_ref, n, src_hbm, dst, sem):
    _for_row_groups(n, lambda r, u: pltpu.make_async_copy(
        _row_tile(src_hbm, idx_ref[0, 0, r]), _row_tile(dst, r), sem).start(priority=u % 2))


def _wait_row_gather(n, src_hbm, dst, sem):
    _for_row_groups(n, lambda r, u: pltpu.make_async_copy(_row_tile(src_hbm, 0), _row_tile(dst, r), sem).wait())


def _combine_body(dest_ref, destn_ref, y_hbm, x_ref, gate_ref, g_ref, b_ref, o_ref, ybuf, sem, *, tc):
    i = pl.program_id(0)
    n = pl.num_programs(0)
    slot = i % 2
    rows = TOP_K * tc

    @pl.when(i == 0)
    def _():
        _start_row_gather(dest_ref, rows, y_hbm, ybuf.at[0], sem.at[0])

    @pl.when(i + 1 < n)
    def _():
        _start_row_gather(destn_ref, rows, y_hbm, ybuf.at[1 - slot], sem.at[1 - slot])

    _wait_row_gather(rows, y_hbm, ybuf.at[slot], sem.at[slot])
    gate = gate_ref[...]
    moe = jnp.zeros(x_ref.shape, F32)
    for k in range(TOP_K):
        moe = moe + gate[:, k:k + 1] * _load_row_tiles(ybuf.at[slot], tc, first=k * tc)
    o_ref[...] = _layer_norm(ALPHA * x_ref[...] + moe, g_ref[...], b_ref[...])


def _combine_call(dest, y, x, gate, g, b, *, tc):
    t, d = x.shape
    n = t // tc
    rows = TOP_K * tc
    dest3 = dest.reshape(n, tc, TOP_K).transpose(0, 2, 1).reshape(n, 1, rows)
    smem_row = lambda f: pl.BlockSpec((1, 1, rows), f, memory_space=pltpu.SMEM)
    row = lambda w: pl.BlockSpec((tc, w), lambda i: (i, 0))
    const = lambda a: pl.BlockSpec(a.shape, lambda i: (0, 0))
    return pl.pallas_call(
        functools.partial(_combine_body, tc=tc),
        out_shape=jax.ShapeDtypeStruct((t, d), F32),
        grid=(n,),
        in_specs=[smem_row(lambda i: (i, 0, 0)), smem_row(lambda i: (jnp.minimum(i + 1, n - 1), 0, 0)),
                  pl.BlockSpec(memory_space=pl.ANY), row(d), row(LANE), const(g), const(b)],
        out_specs=row(d),
        scratch_shapes=[pltpu.VMEM((2, rows * ROW_TILE, LANE), F32), pltpu.SemaphoreType.DMA((2,))],
        compiler_params=_params(("arbitrary",)),
        name="moe_combine",
    )(dest3, dest3, y, x, gate, g, b)


def _head_blocks(w, widths):
    per = sum(widths)
    w = w.reshape(*w.shape[:-1], H_A, per)
    pad = [(0, 0)] * (w.ndim - 1) + [(0, LANE - per)]
    return jnp.pad(w, pad).reshape(*w.shape[:-2], HEAD_COLS)


def _rotate_half_cols(w):
    half = ROPE_DIM // 2
    return jnp.concatenate([-w[..., half:], w[..., :half]], axis=-1)


def _prep_layer(w_in, w_uq, w_ukv, w_oa):
    d = w_in.shape[0]
    splits = (Q_LORA, KV_LORA, ROPE_DIM, 1024, 1024, 1024, D_MODEL, D_MODEL)
    offs = [0]
    for s in splits:
        offs.append(offs[-1] + s)
    lat = w_in[:, offs[0]:offs[2]]
    kr = w_in[:, offs[2]:offs[3]]
    rest = w_in[:, offs[3]:]

    def rope_block(w):
        return jnp.pad(w, ((0, 0), (NOPE_DIM, LANE - NOPE_DIM - ROPE_DIM)))

    w1 = jnp.concatenate([lat, rope_block(kr), rope_block(_rotate_half_cols(kr)), rest], axis=1).astype(BF16)
    assert w1.shape == (d, C_END)
    wvt = w_in[:, offs[5]:offs[6]].T.astype(BF16)

    uq = w_uq.reshape(Q_LORA, H_A, NOPE_DIM + ROPE_DIM)
    uq_rot = jnp.concatenate([jnp.zeros((Q_LORA, H_A, NOPE_DIM), F32), _rotate_half_cols(uq[..., NOPE_DIM:])], -1)
    wuq = jnp.concatenate([_head_blocks(uq.reshape(Q_LORA, -1), (NOPE_DIM + ROPE_DIM,)),
                           _head_blocks(uq_rot.reshape(Q_LORA, -1), (NOPE_DIM + ROPE_DIM,))], axis=1).astype(BF16)

    ukv = w_ukv.reshape(KV_LORA, H_A, NOPE_DIM + V_DIM_A)
    wuk = _head_blocks(ukv[..., :NOPE_DIM].reshape(KV_LORA, -1), (NOPE_DIM,)).astype(BF16)
    wuv = _head_blocks(ukv[..., NOPE_DIM:].reshape(KV_LORA, -1), (V_DIM_A,)).astype(BF16)
    wuvt = ukv[..., NOPE_DIM:].reshape(KV_LORA, H_A * V_DIM_A).T.astype(BF16)

    woa = jnp.pad(w_oa.reshape(H_A, V_DIM_A, D_MODEL), ((0, 0), (0, LANE - V_DIM_A), (0, 0)))
    woa = woa.reshape(HEAD_COLS, D_MODEL).astype(BF16)
    return w1, wvt, wuq, wuk, wuv, wuvt, woa


def _rope_tables(pos):
    half = ROPE_DIM // 2
    inv = ROPE_THETA ** (-jnp.arange(half, dtype=F32) / half)
    ang = pos.astype(F32)[:, None] * inv[None, :]
    cos, sin = jnp.cos(ang), jnp.sin(ang)
    n = pos.shape[0]
    tail = jnp.zeros((n, LANE - NOPE_DIM - ROPE_DIM), F32)
    cos_t = jnp.concatenate([jnp.ones((n, NOPE_DIM), F32), cos, cos, tail], axis=1)
    sin_t = jnp.concatenate([jnp.zeros((n, NOPE_DIM), F32), sin, sin, tail], axis=1)
    return cos_t, sin_t


def _rope_block_of(kr):
    return jnp.pad(kr, ((0, 0), (NOPE_DIM, LANE - NOPE_DIM - ROPE_DIM)))


def _moe_block_rows(t):
    return 256 if t * TOP_K >= 256 * N_EXPERTS else 128


def _moe_row_buffer(t):
    bm = _moe_block_rows(t)
    return jnp.zeros(((-(-t * TOP_K // bm) + N_EXPERTS) * bm * ROW_TILE, LANE), F32)


def _moe_layer(x1, x1_tiles, top_idx, gate, rows_buf, wgu, bgu, wdn, bdn, g, b, *, layer):
    t = x1.shape[0]
    bm = _moe_block_rows(t)
    tc = _pick_tile(t, 128)
    block_e, n_used, dest = _route(top_idx, bm)
    xs = _dispatch_call(dest, x1_tiles, rows_buf, td=_pick_tile(t, 512))
    y = _moe_call(block_e, n_used, xs, wgu, bgu, wdn, bdn, layer=layer, bm=bm)
    return _combine_call(dest, y, x1, gate, g, b, tc=tc), y


def kernel(x_prompt, x_sample, cache_mla_latent, cache_mla_krope, cache_diff_k, cache_diff_v,
           ln_in_g, ln_in_b, w_in, q_norm_g, kv_norm_g, w_uq, w_ukv, lam_q1, lam_k1, lam_q2, lam_k2,
           subln_g, w_oa, w_ob, w_out, ln1_g, ln1_b, w_router, b_router, w_gu, b_gu, w_dn, b_dn,
           ln2_g, ln2_b):
    bp, sp, d = x_prompt.shape
    bs, ss, _ = x_sample.shape
    past = cache_mla_latent.shape[2]
    assert bp == 1 and d == D_MODEL
    tp, ts = bp * sp, bs * ss

    cos_p, sin_p = _rope_tables(jnp.arange(sp, dtype=I32))
    cos_s, sin_s = _rope_tables(jnp.tile(past + jnp.arange(ss, dtype=I32), bs))
    slopes = 2.0 ** (-8.0 * (jnp.arange(H_B, dtype=F32) + 1.0) / H_B)
    slopes = jnp.broadcast_to(slopes[:, None, None], (H_B, 1, LANE))
    tk = _pick_tile(sp, 256)
    tq_mla = _pick_tile(sp, 512)
    tq_diff = _pick_tile(sp, 256)

    xp = _ln_call(x_prompt.reshape(tp, d), ln_in_g, ln_in_b)
    xs = _ln_call(x_sample.reshape(ts, d), ln_in_g, ln_in_b)
    rows_p, rows_s = _moe_row_buffer(tp), _moe_row_buffer(ts)
    wgu, wdn = w_gu.astype(BF16), w_dn.astype(BF16)
    diff_kt = jnp.transpose(cache_diff_k, (0, 1, 3, 4, 2)).reshape(DEPTH * bs * HEAD_COLS, past)
    diff_v = cache_diff_v.reshape(DEPTH * bs * past, HEAD_COLS)
    new_p = ([], [], [], [])
    new_s = ([], [], [], [])
    for l in range(DEPTH):
        lam_init = 0.8 - 0.6 * math.exp(-0.3 * l)
        w1, wvt, wuq, wuk, wuv, wuvt, woa = _prep_layer(w_in[l], w_uq[l], w_ukv[l], w_oa[l])
        gq = q_norm_g[l].reshape(1, Q_LORA)
        gkv = kv_norm_g[l].reshape(1, KV_LORA)
        lamv = jnp.stack([lam_q1[l], lam_k1[l], lam_q2[l], lam_k2[l]])
        subg = subln_g[l].reshape(1, 2 * DH_B)
        wob = w_ob[l].astype(BF16)
        wout = w_out[l].astype(BF16)
        g1, b1 = ln1_g[l].reshape(1, d), ln1_b[l].reshape(1, d)
        g2, b2 = ln2_g[l].reshape(1, d), ln2_b[l].reshape(1, d)
        wr_hi = w_router[l].astype(BF16)
        wr = jnp.concatenate([wr_hi, (w_router[l] - wr_hi.astype(F32)).astype(BF16)], axis=1)
        br = b_router[l].reshape(1, N_EXPERTS)
        diff_extra = [slopes, lamv, subg]

        def diff_specs(head_of):
            return [pl.BlockSpec((1, 1, LANE), lambda a, b: (head_of(a, b), 0, 0)),
                    pl.BlockSpec(lamv.shape, lambda a, b: (0, 0)), pl.BlockSpec(subg.shape, lambda a, b: (0, 0))]

        qa, lat, krb, qd, kd32, kd16, vd32, _, vdt, sga, sgb = _inproj_call(xp, w1, wvt, wuq, gq, gkv, cos_p, sin_p)
        ka, vat = _kvup_call(lat, krb, wuk, wuvt, feature_major_v=True)
        oa = _prompt_attn_call(_mla_prompt_body, qa, ka, vat, [], [], maps=1, tq=tq_mla, tk=tk, unroll=MLA_UNROLL,
                               name="mla_prompt")
        ob = _prompt_attn_call(functools.partial(_diff_prompt_body, lam_init=lam_init), qd, kd16, vdt,
                               diff_extra, diff_specs(lambda h, i: h), maps=2, tq=tq_diff, tk=tk, unroll=DIFF_UNROLL,
                               name="diff_prompt")
        xp1, xp1t, idx_p, gate_p = _outproj_call(oa, ob, sga, sgb, xp, woa, wob, wout, g1, b1, wr, br)
        new_p[0].append(lat.reshape(bp, sp, KV_LORA))
        new_p[1].append(krb[:, NOPE_DIM:NOPE_DIM + ROPE_DIM].reshape(bp, sp, ROPE_DIM))
        new_p[2].append(kd32.reshape(bp, sp, 2 * H_B, DH_B))
        new_p[3].append(vd32.reshape(bp, sp, H_B, 2 * DH_B))

        qa, lat, krb, qd, kd32, kd16, vd32, vd16, _, sga, sgb = _inproj_call(xs, w1, wvt, wuq, gq, gkv, cos_s, sin_s)
        kn, vn = _kvup_call(lat, krb, wuk, wuv, feature_major_v=False)
        kpast, vpast = _kvup_call(cache_mla_latent.reshape(DEPTH * bs * past, KV_LORA),
                                  _rope_block_of(cache_mla_krope[l].reshape(bs * past, ROPE_DIM)), wuk, wuv,
                                  feature_major_v=False, lat_first_row=l * bs * past)
        token_major = pl.BlockSpec((past, LANE), lambda b, h: (b, h))
        oa = _sample_attn_call(_mla_sample_body, qa, kpast, vpast, kn, vn, [], [], batch=bs, past=past,
                               kp_spec=token_major, vp_spec=token_major, name="mla_sample")
        ob = _sample_attn_call(functools.partial(_diff_sample_body, lam_init=lam_init), qd, diff_kt, diff_v, kd16, vd16,
                               diff_extra, diff_specs(lambda b, h: h), batch=bs, past=past,
                               kp_spec=pl.BlockSpec((LANE, past), lambda b, h, l=l: ((l * bs + b) * H_B + h, 0)),
                               vp_spec=pl.BlockSpec((past, LANE), lambda b, h, l=l: (l * bs + b, h)),
                               name="diff_sample")
        xs1, xs1t, idx_s, gate_s = _outproj_call(oa, ob, sga, sgb, xs, woa, wob, wout, g1, b1, wr, br)
        new_s[0].append(lat.reshape(bs, ss, KV_LORA))
        new_s[1].append(krb[:, NOPE_DIM:NOPE_DIM + ROPE_DIM].reshape(bs, ss, ROPE_DIM))
        new_s[2].append(kd32.reshape(bs, ss, 2 * H_B, DH_B))
        new_s[3].append(vd32.reshape(bs, ss, H_B, 2 * DH_B))

        xp, rows_p = _moe_layer(xp1, xp1t, idx_p, gate_p, rows_p, wgu, b_gu, wdn, b_dn, g2, b2, layer=l)
        xs, rows_s = _moe_layer(xs1, xs1t, idx_s, gate_s, rows_s, wgu, b_gu, wdn, b_dn, g2, b2, layer=l)

    return (xp.reshape(bp, sp, d), xs.reshape(bs, ss, d),
            jnp.stack(new_p[0]), jnp.stack(new_p[1]), jnp.stack(new_p[2]), jnp.stack(new_p[3]),
            jnp.stack(new_s[0]), jnp.stack(new_s[1]), jnp.stack(new_s[2]), jnp.stack(new_s[3]))
```
